```python
import math, functools
import jax, jax.numpy as jnp
from jax import lax
import numpy as np

D_MODEL = 1024
BATCH = 8
SEQ = 2048
DEPTH = 2
DEC_BATCH = 32
DEC_SEQ = 8
PAST_LEN = 16384
PAGE_SIZE = 128

N_HEADS = 8
HEAD_DIM = 64
V_DIM = 2 * HEAD_DIM
QK_W = N_HEADS * 2 * HEAD_DIM
V_W = N_HEADS * V_DIM
ATTN_SCALE = HEAD_DIM ** -0.5
Q_BLOCK = 128
POOL_W = D_MODEL
POOL_WINDOWS = (2, 4, 8, 16)
N_POOL_GROUPS = len(POOL_WINDOWS)
POOL_GROUP = POOL_W // N_POOL_GROUPS
POOL_PAD = max(POOL_WINDOWS) - 1
IN_W = POOL_W + 2 * QK_W + V_W
D_FF = 2816
CONV_W = 3
NORM_EPS = 1e-6

kernel_name = 'hybrid_pool_diffattn_convffn_step'


def rmsnorm(x, g):
    xf = x.astype(jnp.float32)
    xf = xf * lax.rsqrt(jnp.mean(xf * xf, axis=-1, keepdims=True) + NORM_EPS)
    return (xf * g.astype(jnp.float32)).astype(x.dtype)


def alibi_slopes():
    return jnp.exp2(-8.0 * jnp.arange(1, N_HEADS + 1, dtype=jnp.float32) / N_HEADS)


def lambda_init(layer):
    return 0.8 - 0.6 * math.exp(-0.3 * layer)


def diff_lambda(lq1, lk1, lq2, lk2, layer):
    f32 = jnp.float32
    return (jnp.exp(jnp.sum(lq1.astype(f32) * lk1.astype(f32)))
            - jnp.exp(jnp.sum(lq2.astype(f32) * lk2.astype(f32))) + lambda_init(layer))


def pool_mixer(u_ext, pos, w_grp, scale):
    B = u_ext.shape[0]
    L = pos.shape[0]
    P = POOL_PAD
    cs = jnp.cumsum(u_ext.astype(jnp.float32), axis=1)
    cs = jnp.pad(cs, ((0, 0), (1, 0), (0, 0)))
    posf = pos.astype(jnp.float32)
    outs = []
    for g, w in enumerate(POOL_WINDOWS):
        sl = slice(g * POOL_GROUP, (g + 1) * POOL_GROUP)
        win = cs[:, P + 1:P + 1 + L, sl] - cs[:, P + 1 - w:P + 1 - w + L, sl]
        cnt = jnp.minimum(posf + 1.0, float(w))[None, :, None]
        outs.append(win / cnt)
    pooled = jnp.concatenate(outs, axis=-1)
    d = (pooled - u_ext[:, P:].astype(jnp.float32)).astype(u_ext.dtype)
    d = d.reshape(B, L, N_POOL_GROUPS, POOL_GROUP)
    y = jnp.einsum('blgc,gce->blge', d, w_grp).reshape(B, L, POOL_W)
    return y * scale


def diff_attention(q, k, v, qpos, kpos, lam):
    s = jnp.einsum('...qhcd,...khcd->...hcqk', q, k).astype(jnp.float32) * ATTN_SCALE
    dist = (qpos[:, None] - kpos[None, :]).astype(jnp.float32)
    s = s - alibi_slopes()[:, None, None, None] * dist
    s = jnp.where(dist >= 0.0, s, -jnp.inf)
    p = jax.nn.softmax(s, axis=-1)
    a = p[..., 0, :, :] - lam * p[..., 1, :, :]
    return jnp.einsum('...hqk,...khv->...qhv', a.astype(v.dtype), v)


def prompt_attention(q, k, v, lam):
    B, S = q.shape[0], q.shape[1]
    nblk = S // Q_BLOCK
    qb = q.reshape(B, nblk, Q_BLOCK, N_HEADS, 2, HEAD_DIM).transpose(1, 0, 2, 3, 4, 5)
    kpos = jnp.arange(S)

    def body(args):
        i, qi = args
        qpos = i * Q_BLOCK + jnp.arange(Q_BLOCK)
        return diff_attention(qi, k, v, qpos, kpos, lam)

    o = lax.map(body, (jnp.arange(nblk), qb))
    return o.transpose(1, 0, 2, 3, 4).reshape(B, S, N_HEADS, V_DIM)


def sample_attention(q, k_new, v_new, cache_k, cache_v, layer, page_table, lam):
    past = page_table.shape[1] * PAGE_SIZE
    L = q.shape[1]
    qpos = past + jnp.arange(L)
    kpos = jnp.arange(past + L)

    def body(args):
        qb, kb, vb, pt = args
        kp = cache_k[layer, pt].reshape(past, N_HEADS, 2, HEAD_DIM).astype(kb.dtype)
        vp = cache_v[layer, pt].reshape(past, N_HEADS, V_DIM).astype(vb.dtype)
        kk = jnp.concatenate([kp, kb], axis=0)
        vv = jnp.concatenate([vp, vb], axis=0)
        return diff_attention(qb, kk, vv, qpos, kpos, lam)

    return lax.map(body, (q, k_new, v_new, page_table))


def conv_ffn(h, prev, w_up, conv_w, conv_b, w_down):
    L = h.shape[1]
    up = h @ w_up
    gpre, val = up[..., :D_FF], up[..., D_FF:]
    ext = jnp.concatenate([prev.astype(gpre.dtype), gpre], axis=1)
    c = conv_b + sum(conv_w[j] * ext[:, j:j + L] for j in range(CONV_W))
    f = (jax.nn.gelu(c, approximate=True) * val) @ w_down
    return f, ext[:, -(CONV_W - 1):]


def setup_inputs(seed: int = 0) -> dict:
    key = jax.random.key(seed)
    ks = iter(jax.random.split(key, 40))
    f32 = jnp.float32

    def nrm(shape, scale):
        return jax.random.normal(next(ks), shape, f32) * scale

    def gain(shape):
        return 1.0 + nrm(shape, 0.05)

    n_pages = PAST_LEN // PAGE_SIZE
    n_used = DEC_BATCH * n_pages
    n_pool = n_used + max(1, n_used // 4)
    perm = jax.random.permutation(next(ks), n_pool)
    page_table = perm[:n_used].reshape(DEC_BATCH, n_pages).astype(jnp.int32)

    return {
        'x_prompt': nrm((BATCH, SEQ, D_MODEL), 1.0),
        'x_sample': nrm((DEC_BATCH, DEC_SEQ, D_MODEL), 1.0),
        'cache_k': nrm((DEPTH, n_pool, PAGE_SIZE, N_HEADS, 2, HEAD_DIM), 1.0),
        'cache_v': nrm((DEPTH, n_pool, PAGE_SIZE, N_HEADS, V_DIM), 1.0),
        'state_pool': nrm((DEPTH, DEC_BATCH, POOL_PAD, POOL_W), 1.0),
        'state_ffn': nrm((DEPTH, DEC_BATCH, CONV_W - 1, D_FF), 1.0),
        'page_table': page_table,
        'norm_pre_mix': gain((DEPTH, D_MODEL)),
        'w_in': nrm((DEPTH, D_MODEL, IN_W), D_MODEL ** -0.5),
        'w_pool_group': nrm((DEPTH, N_POOL_GROUPS, POOL_GROUP, POOL_GROUP), POOL_GROUP ** -0.5),
        'pool_scale': gain((DEPTH, POOL_W)),
        'lambda_q1': nrm((DEPTH, HEAD_DIM), 0.1),
        'lambda_k1': nrm((DEPTH, HEAD_DIM), 0.1),
        'lambda_q2': nrm((DEPTH, HEAD_DIM), 0.1),
        'lambda_k2': nrm((DEPTH, HEAD_DIM), 0.1),
        'subln_gain': gain((DEPTH, V_DIM)),
        'w_gate': nrm((DEPTH, D_MODEL, 2 * D_MODEL), D_MODEL ** -0.5),
        'b_gate': nrm((DEPTH, 2 * D_MODEL), 0.02),
        'w_branch_pool': nrm((DEPTH, POOL_W, D_MODEL), POOL_W ** -0.5),
        'w_branch_attn': nrm((DEPTH, V_W, D_MODEL), V_W ** -0.5),
        'w_out': nrm((DEPTH, D_MODEL, D_MODEL), D_MODEL ** -0.5),
        'norm_post_mix': gain((DEPTH, D_MODEL)),
        'norm_pre_ffn': gain((DEPTH, D_MODEL)),
        'w_up': nrm((DEPTH, D_MODEL, 2 * D_FF), D_MODEL ** -0.5),
        'conv_w': nrm((DEPTH, CONV_W, D_FF), CONV_W ** -0.5),
        'conv_b': nrm((DEPTH, D_FF), 0.02),
        'w_down': nrm((DEPTH, D_FF, D_MODEL), D_FF ** -0.5),
        'norm_post_ffn': gain((DEPTH, D_MODEL)),
    }


def reference(x_prompt, x_sample, cache_k, cache_v, state_pool, state_ffn, page_table,
              norm_pre_mix, w_in, w_pool_group, pool_scale, lambda_q1, lambda_k1, lambda_q2,
              lambda_k2, subln_gain, w_gate, b_gate, w_branch_pool, w_branch_attn, w_out,
              norm_post_mix, norm_pre_ffn, w_up, conv_w, conv_b, w_down, norm_post_ffn):

    def run_layer(l, x, pool_prev, ffn_prev, pos, attend):
        B, L, _ = x.shape
        h = rmsnorm(x, norm_pre_mix[l])
        z = h @ w_in[l]
        u = z[..., :POOL_W]
        q = z[..., POOL_W:POOL_W + QK_W].reshape(B, L, N_HEADS, 2, HEAD_DIM)
        k = z[..., POOL_W + QK_W:POOL_W + 2 * QK_W].reshape(B, L, N_HEADS, 2, HEAD_DIM)
        v = z[..., POOL_W + 2 * QK_W:].reshape(B, L, N_HEADS, V_DIM)
        u_ext = jnp.concatenate([pool_prev.astype(u.dtype), u], axis=1)
        a = pool_mixer(u_ext, pos, w_pool_group[l], pool_scale[l])
        lam = diff_lambda(lambda_q1[l], lambda_k1[l], lambda_q2[l], lambda_k2[l], l)
        o = attend(q, k, v, lam, l)
        o = (rmsnorm(o, subln_gain[l]) * (1.0 - lambda_init(l))).reshape(B, L, V_W)
        g = jax.nn.sigmoid(h @ w_gate[l] + b_gate[l])
        mix = g[..., :D_MODEL] * (a @ w_branch_pool[l]) + g[..., D_MODEL:] * (o @ w_branch_attn[l])
        x = x + rmsnorm(mix @ w_out[l], norm_post_mix[l])
        f, ffn_new = conv_ffn(rmsnorm(x, norm_pre_ffn[l]), ffn_prev, w_up[l], conv_w[l], conv_b[l], w_down[l])
        x = x + rmsnorm(f, norm_post_ffn[l])
        return x, k, v, u_ext[:, -POOL_PAD:], ffn_new

    def attend_prompt(q, k, v, lam, l):
        return prompt_attention(q, k, v, lam)

    def attend_sample(q, k, v, lam, l):
        return sample_attention(q, k, v, cache_k, cache_v, l, page_table, lam)

    b_p, s_p = x_prompt.shape[0], x_prompt.shape[1]
    pos_p = jnp.arange(s_p)
    pos_s = page_table.shape[1] * PAGE_SIZE + jnp.arange(x_sample.shape[1])
    yp, ys = x_prompt, x_sample
    kp_l, vp_l, pp_l, fp_l = [], [], [], []
    ks_l, vs_l, ps_l, fs_l = [], [], [], []
    for l in range(DEPTH):
        pool0 = jnp.zeros((b_p, POOL_PAD, POOL_W), x_prompt.dtype)
        ffn0 = jnp.zeros((b_p, CONV_W - 1, D_FF), x_prompt.dtype)
        yp, kp, vp, pp, fp = run_layer(l, yp, pool0, ffn0, pos_p, attend_prompt)
        ys, ks_, vs_, ps, fs = run_layer(l, ys, state_pool[l], state_ffn[l], pos_s, attend_sample)
        kp_l.append(kp); vp_l.append(vp); pp_l.append(pp); fp_l.append(fp)
        ks_l.append(ks_); vs_l.append(vs_); ps_l.append(ps); fs_l.append(fs)

    k_prompt = jnp.stack(kp_l)
    v_prompt = jnp.stack(vp_l)
    pool_prompt = jnp.stack(pp_l)
    ffn_prompt = jnp.stack(fp_l)
    k_sample = jnp.stack(ks_l)
    v_sample = jnp.stack(vs_l)
    pool_sample = jnp.stack(ps_l)
    ffn_sample = jnp.stack(fs_l)
    return (yp, ys, k_prompt, v_prompt, pool_prompt, ffn_prompt, k_sample, v_sample, pool_sample, ffn_sample)
```

```python
import functools
import math

import jax
import jax.numpy as jnp
from jax import lax
from jax.experimental import pallas as pl
from jax.experimental.pallas import tpu as pltpu

F32 = jnp.float32
BF16 = jnp.bfloat16

D_MODEL = 1024
N_HEADS = 8
HEAD_DIM = 64
V_DIM = 2 * HEAD_DIM
ATTN_SCALE = HEAD_DIM ** -0.5
POOL_WINDOWS = (2, 4, 8, 16)
POOL_GROUP = D_MODEL // len(POOL_WINDOWS)
POOL_PAD = max(POOL_WINDOWS) - 1
POOL_HALO = 16
D_FF = 2816
CONV_W = 3
CONV_HALO = 8
NORM_EPS = 1e-6
PAGE_SIZE = 128
NEG_BIG = -1e30

VMEM_LIMIT_BYTES = 56 * 1024 * 1024

ROW_TILE_PROJ = 512
ROW_TILE_MIX = 512
ROW_TILE_FFN = 256
ATTN_TILE = 256
PAGES_PER_STEP = 4


def _lambda_init(layer):
    return 0.8 - 0.6 * math.exp(-0.3 * layer)


def _rms(x, g):
    ms = jnp.mean(x * x, axis=-1, keepdims=True)
    return x * lax.rsqrt(ms + NORM_EPS) * g


def _const_spec(shape):
    nd = len(shape)
    return pl.BlockSpec(shape, lambda *_: (0,) * nd, pipeline_mode=pl.Buffered(1))


def _diff_lambda(lam_ref, layer):
    lp = lam_ref[...]
    a = jnp.sum(lp[0:1, :] * lp[1:2, :], axis=-1, keepdims=True)
    b = jnp.sum(lp[2:3, :] * lp[3:4, :], axis=-1, keepdims=True)
    return jnp.exp(a) - jnp.exp(b) + _lambda_init(layer)


def _in_proj_kernel(x_ref, prev_ref, gn_ref, win_ref, wpool_ref, pscale_ref,
                    qm_ref, k_ref, v_ref, kb_ref, vb_ref, a_ref, pstate_ref,
                    ext_ref, *, G, T, pos0):
    t = pl.program_id(1)
    M = G * T
    h = _rms(x_ref[...], gn_ref[...]).astype(BF16)

    @pl.when(t == 0)
    def _():
        ext_ref[:, 0:POOL_HALO, :] = prev_ref[...]

    zu = jnp.dot(h, win_ref[:, 0:D_MODEL], preferred_element_type=F32)
    ext_ref[:, POOL_HALO:POOL_HALO + T, :] = zu.reshape(G, T, D_MODEL)

    zq = jnp.dot(h, win_ref[:, D_MODEL:2 * D_MODEL], preferred_element_type=F32) * ATTN_SCALE
    lane = lax.broadcasted_iota(jnp.int32, (1, D_MODEL), 1)
    first_map = (lane % V_DIM) < HEAD_DIM
    qm_ref[0] = jnp.where(first_map, zq, 0.0).astype(BF16)
    qm_ref[1] = jnp.where(first_map, 0.0, zq).astype(BF16)

    zk = jnp.dot(h, win_ref[:, 2 * D_MODEL:3 * D_MODEL], preferred_element_type=F32)
    k_ref[...] = zk
    kb_ref[...] = zk.astype(BF16)
    zv = jnp.dot(h, win_ref[:, 3 * D_MODEL:4 * D_MODEL], preferred_element_type=F32)
    v_ref[...] = zv
    vb_ref[...] = zv.astype(BF16)

    pos = pos0 + t * T + lax.broadcasted_iota(jnp.int32, (1, T, 1), 1)
    for g, w in enumerate(POOL_WINDOWS):
        cols = slice(g * POOL_GROUP, (g + 1) * POOL_GROUP)
        u_new = ext_ref[:, POOL_HALO:POOL_HALO + T, cols]
        win = u_new
        for j in range(1, w):
            win = win + ext_ref[:, POOL_HALO - j:POOL_HALO - j + T, cols]
        cnt = jnp.minimum(pos + 1, w).astype(F32)
        d = (win / cnt - u_new).astype(BF16).reshape(M, POOL_GROUP)
        y = jnp.dot(d, wpool_ref[g], preferred_element_type=F32)
        a_ref[:, cols] = (y * pscale_ref[:, cols]).astype(BF16)

    tail = ext_ref[:, T:T + POOL_HALO, :]
    pstate_ref[...] = tail
    ext_ref[:, 0:POOL_HALO, :] = tail


def _in_proj(x2d, prev, gn, win, wpool, pscale, *, n_seq, seq_len, G, T, pos0):
    N = x2d.shape[0]
    M = G * T
    tiles_per_seq = seq_len // T
    n_outer = n_seq // G
    row = lambda b, t: (b * tiles_per_seq + t, 0)
    out_shape = (
        jax.ShapeDtypeStruct((2, N, D_MODEL), BF16),
        jax.ShapeDtypeStruct((N, D_MODEL), F32),
        jax.ShapeDtypeStruct((N, D_MODEL), F32),
        jax.ShapeDtypeStruct((N, D_MODEL), BF16),
        jax.ShapeDtypeStruct((N, D_MODEL), BF16),
        jax.ShapeDtypeStruct((N, D_MODEL), BF16),
        jax.ShapeDtypeStruct((n_seq, POOL_HALO, D_MODEL), F32),
    )
    return pl.pallas_call(
        functools.partial(_in_proj_kernel, G=G, T=T, pos0=pos0),
        grid=(n_outer, tiles_per_seq),
        in_specs=[
            pl.BlockSpec((M, D_MODEL), row),
            pl.BlockSpec((G, POOL_HALO, D_MODEL), lambda b, t: (b, 0, 0)),
            _const_spec((1, D_MODEL)),
            _const_spec((D_MODEL, 4 * D_MODEL)),
            _const_spec((len(POOL_WINDOWS), POOL_GROUP, POOL_GROUP)),
            _const_spec((1, D_MODEL)),
        ],
        out_specs=(
            pl.BlockSpec((2, M, D_MODEL), lambda b, t: (0, b * tiles_per_seq + t, 0)),
            pl.BlockSpec((M, D_MODEL), row),
            pl.BlockSpec((M, D_MODEL), row),
            pl.BlockSpec((M, D_MODEL), row),
            pl.BlockSpec((M, D_MODEL), row),
            pl.BlockSpec((M, D_MODEL), row),
            pl.BlockSpec((G, POOL_HALO, D_MODEL), lambda b, t: (b, 0, 0)),
        ),
        out_shape=out_shape,
        scratch_shapes=[pltpu.VMEM((G, POOL_HALO + T, D_MODEL), F32)],
        compiler_params=pltpu.CompilerParams(
            dimension_semantics=("arbitrary", "arbitrary"), vmem_limit_bytes=VMEM_LIMIT_BYTES),
        name="in_proj",
    )(x2d, prev, gn, win, wpool, pscale)


def _prompt_attn_kernel(slope_ref, qm_ref, kb_ref, vb_ref, lam_ref, sg_ref, o_ref, *, layer, TQ):
    hd = pl.program_id(1)
    qi = pl.program_id(2)
    slope = slope_ref[hd]
    q2 = qm_ref[...].reshape(2 * TQ, V_DIM)
    key_iota = lax.broadcasted_iota(jnp.int32, (1, TQ), 1)

    def step(j, carry, masked):
        m, l, acc = carry
        kj = kb_ref[pl.ds(pl.multiple_of(j * TQ, TQ), TQ), :]
        vj = vb_ref[pl.ds(pl.multiple_of(j * TQ, TQ), TQ), :]
        s = lax.dot_general(q2, kj, (((1,), (1,)), ((), ())), preferred_element_type=F32)
        s = s + slope * ((j - qi) * TQ + key_iota).astype(F32)
        if masked:
            qrow = lax.broadcasted_iota(jnp.int32, (2 * TQ, 1), 0) % TQ
            s = jnp.where(key_iota <= qrow, s, NEG_BIG)
        m_new = jnp.maximum(m, jnp.max(s, axis=-1, keepdims=True))
        alpha = jnp.exp(m - m_new)
        p = jnp.exp(s - m_new)
        l = alpha * l + jnp.sum(p, axis=-1, keepdims=True)
        acc = alpha * acc + jnp.dot(p.astype(BF16), vj, preferred_element_type=F32)
        return m_new, l, acc

    init = (jnp.full((2 * TQ, 1), NEG_BIG, F32), jnp.zeros((2 * TQ, 1), F32), jnp.zeros((2 * TQ, V_DIM), F32))
    carry = lax.fori_loop(0, qi, lambda j, c: step(j, c, False), init)
    m, l, acc = step(qi, carry, True)

    lam = _diff_lambda(lam_ref, layer)
    o = acc[0:TQ] / l[0:TQ] - lam * (acc[TQ:2 * TQ] / l[TQ:2 * TQ])
    o = _rms(o, sg_ref[...]) * (1.0 - _lambda_init(layer))
    o_ref[...] = o.astype(o_ref.dtype)


def _prompt_attn(slopes, qm, kb, vb, lam_params, subln_gain, *, layer, n_seq, seq_len):
    N = kb.shape[0]
    TQ = ATTN_TILE
    nq = seq_len // TQ
    return pl.pallas_call(
        functools.partial(_prompt_attn_kernel, layer=layer, TQ=TQ),
        grid=(n_seq, N_HEADS, nq),
        in_specs=[
            pl.BlockSpec(memory_space=pltpu.SMEM),
            pl.BlockSpec((2, TQ, V_DIM), lambda b, h, q: (0, b * nq + q, h)),
            pl.BlockSpec((seq_len, V_DIM), lambda b, h, q: (b, h)),
            pl.BlockSpec((seq_len, V_DIM), lambda b, h, q: (b, h)),
            _const_spec((4, HEAD_DIM)),
            _const_spec((1, V_DIM)),
        ],
        out_specs=pl.BlockSpec((TQ, V_DIM), lambda b, h, q: (b * nq + q, h)),
        out_shape=jax.ShapeDtypeStruct((N, D_MODEL), BF16),
        compiler_params=pltpu.CompilerParams(
            dimension_semantics=("arbitrary", "arbitrary", "arbitrary"), vmem_limit_bytes=VMEM_LIMIT_BYTES),
        name="prompt_attn",
    )(slopes, qm, kb, vb, lam_params, subln_gain)


def _sample_attn_kernel(pt_ref, qbd_ref, rslope_ref, rqpos_ref, knew_ref, vnew_ref, lam_ref, sg_ref, *rest,
                        layer, n_pages_step, past_len, L):
    kpages = rest[:n_pages_step]
    vpages = rest[n_pages_step:2 * n_pages_step]
    o_ref = rest[2 * n_pages_step]
    m_ref, l_ref, acc_ref, knew_pad_ref, vnew_pad_ref = rest[2 * n_pages_step + 1:]
    j = pl.program_id(1)
    R = N_HEADS * 2 * L
    qbd = qbd_ref[...]
    rslope = rslope_ref[...]
    rqpos = rqpos_ref[...]

    @pl.when(j == 0)
    def _():
        m_ref[...] = jnp.full(m_ref.shape, NEG_BIG, F32)
        l_ref[...] = jnp.zeros(l_ref.shape, F32)
        acc_ref[...] = jnp.zeros(acc_ref.shape, F32)

    def update(kblk, vblk, kpos, causal):
        s =lax.dot_general(qbd, kblk, (((1,), (1,)), ((), ())), preferred_element_type=F32)
        dist = (rqpos - kpos).astype(F32)
        s = s - rslope * dist
        if causal:
            s = jnp.where(dist >= 0.0, s, NEG_BIG)
        m_old = m_ref[...]
        m_new = jnp.maximum(m_old, jnp.max(s, axis=-1, keepdims=True))
        alpha = jnp.exp(m_old - m_new)
        p = jnp.exp(s - m_new)
        l_ref[...] = alpha * l_ref[...] + jnp.sum(p, axis=-1, keepdims=True)
        m_ref[...] = m_new
        pb = p.astype(BF16)
        for hd in range(N_HEADS):
            rows = slice(hd * 2 * L, (hd + 1) * 2 * L)
            pv = jnp.dot(pb[rows, :], vblk[:, hd * V_DIM:(hd + 1) * V_DIM], preferred_element_type=F32)
            acc_ref[rows, :] = alpha[rows, :] * acc_ref[rows, :] + pv

    lane = lax.broadcasted_iota(jnp.int32, (1, PAGE_SIZE), 1)
    for p_i in range(n_pages_step):
        kpos = (j * n_pages_step + p_i) * PAGE_SIZE + lane
        update(kpages[p_i][...].astype(BF16), vpages[p_i][...].astype(BF16), kpos, False)

    @pl.when(j == pl.num_programs(1) - 1)
    def _():
        knew_pad_ref[...] = jnp.zeros(knew_pad_ref.shape, F32)
        vnew_pad_ref[...] = jnp.zeros(vnew_pad_ref.shape, F32)
        knew_pad_ref[0:L, :] = knew_ref[...]
        vnew_pad_ref[0:L, :] = vnew_ref[...]
        update(knew_pad_ref[...].astype(BF16), vnew_pad_ref[...].astype(BF16), past_len + lane, True)
        lam = _diff_lambda(lam_ref, layer)
        on = acc_ref[...] / l_ref[...]
        for hd in range(N_HEADS):
            r0 = hd * 2 * L
            o = on[r0:r0 + L, :] - lam * on[r0 + L:r0 + 2 * L, :]
            o = _rms(o, sg_ref[...]) * (1.0 - _lambda_init(layer))
            o_ref[:, hd * V_DIM:(hd + 1) * V_DIM] = o.astype(o_ref.dtype)


def _sample_attn(page_table, qbd, rslope, rqpos, k_new, v_new, lam_params, subln_gain, cache_k4, cache_v4,
                 *, layer, n_seq, L):
    n_pages = page_table.shape[1]
    P = PAGES_PER_STEP
    n_steps = n_pages // P
    R = N_HEADS * 2 * L

    def page_spec(p_i):
        return pl.BlockSpec((None, None, PAGE_SIZE, D_MODEL),
                            lambda b, j, pt: (layer, pt[b, j * P + p_i], 0, 0))

    grid_spec = pltpu.PrefetchScalarGridSpec(
        num_scalar_prefetch=1,
        grid=(n_seq, n_steps),
        in_specs=[
            pl.BlockSpec((None, R, D_MODEL), lambda b, j, pt: (b, 0, 0)),
            pl.BlockSpec((R, 1), lambda b, j, pt: (0, 0)),
            pl.BlockSpec((R, 1), lambda b, j, pt: (0, 0)),
            pl.BlockSpec((L, D_MODEL), lambda b, j, pt: (b, 0)),
            pl.BlockSpec((L, D_MODEL), lambda b, j, pt: (b, 0)),
            pl.BlockSpec((4, HEAD_DIM), lambda b, j, pt: (0, 0)),
            pl.BlockSpec((1, V_DIM), lambda b, j, pt: (0, 0)),
        ] + [page_spec(p_i) for p_i in range(P)] + [page_spec(p_i) for p_i in range(P)],
        out_specs=pl.BlockSpec((L, D_MODEL), lambda b, j, pt: (b, 0)),
        scratch_shapes=[pltpu.VMEM((R, 1), F32), pltpu.VMEM((R, 1), F32), pltpu.VMEM((R, V_DIM), F32),
                        pltpu.VMEM((PAGE_SIZE, D_MODEL), F32), pltpu.VMEM((PAGE_SIZE, D_MODEL), F32)],
    )
    return pl.pallas_call(
        functools.partial(_sample_attn_kernel, layer=layer, n_pages_step=P, past_len=n_pages * PAGE_SIZE, L=L),
        grid_spec=grid_spec,
        out_shape=jax.ShapeDtypeStruct((n_seq * L, D_MODEL), F32),
        compiler_params=pltpu.CompilerParams(
            dimension_semantics=("arbitrary", "arbitrary"), vmem_limit_bytes=VMEM_LIMIT_BYTES),
        name="sample_attn",
    )(page_table, qbd, rslope, rqpos, k_new, v_new, lam_params, subln_gain,
      *([cache_k4] * P), *([cache_v4] * P))


def _mix_kernel(x_ref, a_ref, o_ref, gn_ref, wg_ref, bg_ref, wbp_ref, wba_ref, wout_ref, gpost_ref, y_ref):
    x = x_ref[...]
    h = _rms(x, gn_ref[...]).astype(BF16)
    g = jax.nn.sigmoid(jnp.dot(h, wg_ref[...], preferred_element_type=F32) + bg_ref[...])
    bp = jnp.dot(a_ref[...].astype(BF16), wbp_ref[...], preferred_element_type=F32)
    ba = jnp.dot(o_ref[...].astype(BF16), wba_ref[...], preferred_element_type=F32)
    mix = g[:, 0:D_MODEL] * bp + g[:, D_MODEL:2 * D_MODEL] * ba
    y = jnp.dot(mix.astype(BF16), wout_ref[...], preferred_element_type=F32)
    y_ref[...] = x + _rms(y, gpost_ref[...])


def _mix(x2d, a, o, gn, wg, bg, wbp, wba, wout, gpost, *, tile):
    N = x2d.shape[0]
    row = lambda i: (i, 0)
    return pl.pallas_call(
        _mix_kernel,
        grid=(N // tile,),
        in_specs=[
            pl.BlockSpec((tile, D_MODEL), row),
            pl.BlockSpec((tile, D_MODEL), row),
            pl.BlockSpec((tile, D_MODEL), row),
            _const_spec((1, D_MODEL)),
            _const_spec((D_MODEL, 2 * D_MODEL)),
            _const_spec((1, 2 * D_MODEL)),
            _const_spec((D_MODEL, D_MODEL)),
            _const_spec((D_MODEL, D_MODEL)),
            _const_spec((D_MODEL, D_MODEL)),
            _const_spec((1, D_MODEL)),
        ],
        out_specs=pl.BlockSpec((tile, D_MODEL), row),
        out_shape=jax.ShapeDtypeStruct((N, D_MODEL), F32),
        compiler_params=pltpu.CompilerParams(
            dimension_semantics=("arbitrary",), vmem_limit_bytes=VMEM_LIMIT_BYTES),
        name="mix",
    )(x2d, a, o, gn, wg, bg, wbp, wba, wout, gpost)


def _ffn_kernel(x_ref, prev_ref, gn_ref, wup_ref, cw_ref, cb_ref, wdown_ref, gpost_ref,
                y_ref, fstate_ref, ext_ref, *, G, T):
    t = pl.program_id(1)
    M = G * T
    x = x_ref[...]
    hn = _rms(x, gn_ref[...]).astype(BF16)

    @pl.when(t == 0)
    def _():
        ext_ref[:, 0:CONV_HALO, :] = prev_ref[...]

    gpre = jnp.dot(hn, wup_ref[:, 0:D_FF], preferred_element_type=F32)
    ext_ref[:, CONV_HALO:CONV_HALO + T, :] = gpre.reshape(G, T, D_FF)
    val = jnp.dot(hn, wup_ref[:, D_FF:2 * D_FF], preferred_element_type=F32)

    cw = cw_ref[...]
    c = cb_ref[...].reshape(1, 1, D_FF)
    for j in range(CONV_W):
        off = CONV_HALO - (CONV_W - 1) + j
        c = c + cw[j:j + 1, :].reshape(1, 1, D_FF) * ext_ref[:, off:off + T, :]
    act = jax.nn.gelu(c, approximate=True).reshape(M, D_FF) * val
    f = jnp.dot(act.astype(BF16), wdown_ref[...], preferred_element_type=F32)
    y_ref[...] = x + _rms(f, gpost_ref[...])

    tail = ext_ref[:, T:T + CONV_HALO, :]
    fstate_ref[...] = tail
    ext_ref[:, 0:CONV_HALO, :] = tail


def _ffn(x2d, prev, gn, wup, cw, cb, wdown, gpost, *, n_seq, seq_len, G, T):
    N = x2d.shape[0]
    M = G * T
    tiles_per_seq = seq_len // T
    n_outer = n_seq // G
    row = lambda b, t: (b * tiles_per_seq + t, 0)
    return pl.pallas_call(
        functools.partial(_ffn_kernel, G=G, T=T),
        grid=(n_outer, tiles_per_seq),
        in_specs=[
            pl.BlockSpec((M, D_MODEL), row),
            pl.BlockSpec((G, CONV_HALO, D_FF), lambda b, t: (b, 0, 0)),
            _const_spec((1, D_MODEL)),
            _const_spec((D_MODEL, 2 * D_FF)),
            _const_spec((CONV_W, D_FF)),
            _const_spec((1, D_FF)),
            _const_spec((D_FF, D_MODEL)),
            _const_spec((1, D_MODEL)),
        ],
        out_specs=(
            pl.BlockSpec((M, D_MODEL), row),
            pl.BlockSpec((G, CONV_HALO, D_FF), lambda b, t: (b, 0, 0)),
        ),
        out_shape=(
            jax.ShapeDtypeStruct((N, D_MODEL), F32),
            jax.ShapeDtypeStruct((n_seq, CONV_HALO, D_FF), F32),
        ),
        scratch_shapes=[pltpu.VMEM((G, CONV_HALO + T, D_FF), F32)],
        compiler_params=pltpu.CompilerParams(
            dimension_semantics=("arbitrary", "arbitrary"), vmem_limit_bytes=VMEM_LIMIT_BYTES),
        name="ffn",
    )(x2d, prev, gn, wup, cw, cb, wdown, gpost)


def _block_diag_queries(qm, n_seq, L):
    q = qm.reshape(2, n_seq, L, N_HEADS, V_DIM)
    eye = jnp.eye(N_HEADS, dtype=qm.dtype)
    qbd = jnp.einsum('cbihd,hg->bgcihd', q, eye)
    return qbd.reshape(n_seq, N_HEADS * 2 * L, D_MODEL)


def kernel(x_prompt, x_sample, cache_k, cache_v, state_pool, state_ffn, page_table, norm_pre_mix, w_in,
           w_pool_group, pool_scale, lambda_q1, lambda_k1, lambda_q2, lambda_k2, subln_gain, w_gate, b_gate,
           w_branch_pool, w_branch_attn, w_out, norm_post_mix, norm_pre_ffn, w_up, conv_w, conv_b, w_down,
           norm_post_ffn):
    depth = w_in.shape[0]
    n_p, s_p, _ = x_prompt.shape
    n_s, s_s, _ = x_sample.shape
    n_pages = page_table.shape[1]
    past_len = n_pages * PAGE_SIZE
    n_pool = cache_k.shape[1]

    cache_k4 = cache_k.reshape(depth, n_pool, PAGE_SIZE, D_MODEL)
    cache_v4 = cache_v.reshape(depth, n_pool, PAGE_SIZE, D_MODEL)
    slopes = jnp.exp2(-8.0 * jnp.arange(1, N_HEADS + 1, dtype=F32) / N_HEADS)
    rslope = jnp.repeat(slopes, 2 * s_s).reshape(N_HEADS * 2 * s_s, 1)
    rqpos = jnp.tile(past_len + jnp.arange(s_s, dtype=jnp.int32), N_HEADS * 2).reshape(N_HEADS * 2 * s_s, 1)

    pool_prev_p = jnp.zeros((n_p, POOL_HALO, D_MODEL), F32)
    ffn_prev_p = jnp.zeros((n_p, CONV_HALO, D_FF), F32)
    pool_prev_s = jnp.pad(state_pool, ((0, 0), (0, 0), (POOL_HALO - POOL_PAD, 0), (0, 0)))
    ffn_prev_s = jnp.pad(state_ffn, ((0, 0), (0, 0), (CONV_HALO - (CONV_W - 1), 0), (0, 0)))

    xp = x_prompt.reshape(n_p * s_p, D_MODEL)
    xs = x_sample.reshape(n_s * s_s, D_MODEL)
    outs = {name: [] for name in ("kp", "vp", "pp", "fp", "ks", "vs", "ps", "fs")}

    for l in range(depth):
        gn = norm_pre_mix[l].reshape(1, D_MODEL)
        win = w_in[l].astype(BF16)
        wpool = w_pool_group[l].astype(BF16)
        pscale = pool_scale[l].reshape(1, D_MODEL)
        lam_params = jnp.stack([lambda_q1[l], lambda_k1[l], lambda_q2[l], lambda_k2[l]])
        sg = subln_gain[l].reshape(1, V_DIM)
        mix_w = (gn, w_gate[l].astype(BF16), b_gate[l].reshape(1, 2 * D_MODEL), w_branch_pool[l].astype(BF16),
                 w_branch_attn[l].astype(BF16), w_out[l].astype(BF16), norm_post_mix[l].reshape(1, D_MODEL))
        ffn_w = (norm_pre_ffn[l].reshape(1, D_MODEL), w_up[l].astype(BF16), conv_w[l], conv_b[l].reshape(1, D_FF),
                 w_down[l].astype(BF16), norm_post_ffn[l].reshape(1, D_MODEL))

        qm, k, v, kb, vb, a, pstate = _in_proj(xp, pool_prev_p, gn, win, wpool, pscale, n_seq=n_p, seq_len=s_p,
                                               G=1, T=ROW_TILE_PROJ, pos0=0)
        o = _prompt_attn(slopes, qm, kb, vb, lam_params, sg, layer=l, n_seq=n_p, seq_len=s_p)
        x1 = _mix(xp, a, o, *mix_w, tile=ROW_TILE_MIX)
        xp, fstate = _ffn(x1, ffn_prev_p, *ffn_w, n_seq=n_p, seq_len=s_p, G=1, T=ROW_TILE_FFN)
        outs["kp"].append(k.reshape(n_p, s_p, N_HEADS, 2, HEAD_DIM))
        outs["vp"].append(v.reshape(n_p, s_p, N_HEADS, V_DIM))
        outs["pp"].append(pstate[:, POOL_HALO - POOL_PAD:, :])
        outs["fp"].append(fstate[:, CONV_HALO - (CONV_W - 1):, :])

        qm, k, v, _, _, a, pstate = _in_proj(xs, pool_prev_s[l], gn, win, wpool, pscale, n_seq=n_s, seq_len=s_s,
                                             G=n_s, T=s_s, pos0=past_len)
        qbd = _block_diag_queries(qm, n_s, s_s)
        o = _sample_attn(page_table, qbd, rslope, rqpos, k, v, lam_params, sg, cache_k4, cache_v4,
                         layer=l, n_seq=n_s, L=s_s)
        x1 = _mix(xs, a, o, *mix_w, tile=n_s * s_s)
        xs, fstate = _ffn(x1, ffn_prev_s[l], *ffn_w, n_seq=n_s, seq_len=s_s, G=n_s, T=s_s)
        outs["ks"].append(k.reshape(n_s, s_s, N_HEADS, 2, HEAD_DIM))
        outs["vs"].append(v.reshape(n_s, s_s, N_HEADS, V_DIM))
        outs["ps"].append(pstate[:, POOL_HALO - POOL_PAD:, :])
        outs["fs"].append(fstate[:, CONV_HALO - (CONV_W - 1):, :])

    st = {name: jnp.stack(v) for name, v in outs.items()}
    return (xp.reshape(n_p, s_p, D_MODEL), xs.reshape(n_s, s_s, D_MODEL),
            st["kp"], st["vp"], st["pp"], st["fp"], st["ks"], st["vs"], st["ps"], st["fs"])
```

```python
import functools
import math

import jax
import jax.numpy as jnp
from jax import lax
from jax.experimental import pallas as pl
from jax.experimental.pallas import tpu as pltpu

F32 = jnp.float32
BF16 = jnp.bfloat16

D_MODEL = 1024
N_HEADS = 8
HEAD_DIM = 64
V_DIM = 2 * HEAD_DIM
ATTN_SCALE = HEAD_DIM ** -0.5
LOG2E = math.log2(math.e)
POOL_WINDOWS = (2, 4, 8, 16)
POOL_GROUP = D_MODEL // len(POOL_WINDOWS)
POOL_PAD = max(POOL_WINDOWS) - 1
POOL_HALO = 16
D_FF = 2816
CONV_W = 3
CONV_HALO = 8
NORM_EPS = 1e-6
PAGE_SIZE = 128
NEG_BIG = -1e30

VMEM_LIMIT_BYTES = 56 * 1024 * 1024

ROW_TILE_PROJ = 512
ROW_TILE_MIX = 512
ROW_TILE_FFN = 256
ATTN_TILE = 256
PAGES_PER_STEP = 8


def _lambda_init(layer):
    return 0.8 - 0.6 * math.exp(-0.3 * layer)


def _rms(x, g):
    ms = jnp.mean(x * x, axis=-1, keepdims=True)
    return x * lax.rsqrt(ms + NORM_EPS) * g


def _const_spec(shape):
    nd = len(shape)
    return pl.BlockSpec(shape, lambda *_: (0,) * nd, pipeline_mode=pl.Buffered(1))


def _diff_lambda(lam_ref, layer):
    lp = lam_ref[...]
    a = jnp.sum(lp[0:1, :] * lp[1:2, :], axis=-1, keepdims=True)
    b = jnp.sum(lp[2:3, :] * lp[3:4, :], axis=-1, keepdims=True)
    return jnp.exp(a) - jnp.exp(b) + _lambda_init(layer)


def _store_queries(h, win_ref, qm_ref):
    zq = jnp.dot(h, win_ref[:, D_MODEL:2 * D_MODEL], preferred_element_type=F32) * (ATTN_SCALE * LOG2E)
    lane = lax.broadcasted_iota(jnp.int32, (1, D_MODEL), 1)
    first_map = (lane % V_DIM) < HEAD_DIM
    qm_ref[0] = jnp.where(first_map, zq, 0.0).astype(BF16)
    qm_ref[1] = jnp.where(first_map, 0.0, zq).astype(BF16)


def _pool_branch(t, h, prev_ref, win_ref, wpool_ref, pscale_ref, a_ref, pstate_ref, ext_ref, *, G, T, pos0):
    M = G * T

    @pl.when(t == 0)
    def _():
        ext_ref[:, 0:POOL_HALO, :] = prev_ref[...]

    zu = jnp.dot(h, win_ref[:, 0:D_MODEL], preferred_element_type=F32)
    ext_ref[:, POOL_HALO:POOL_HALO + T, :] = zu.reshape(G, T, D_MODEL)

    pos = pos0 + t * T + lax.broadcasted_iota(jnp.int32, (1, T, 1), 1)
    for g, w in enumerate(POOL_WINDOWS):
        cols = slice(g * POOL_GROUP, (g + 1) * POOL_GROUP)
        u_new = ext_ref[:, POOL_HALO:POOL_HALO + T, cols]
        win = u_new
        for j in range(1, w):
            win = win + ext_ref[:, POOL_HALO - j:POOL_HALO - j + T, cols]
        cnt = jnp.minimum(pos + 1, w).astype(F32)
        d = (win / cnt - u_new).astype(BF16).reshape(M, POOL_GROUP)
        y = jnp.dot(d, wpool_ref[g], preferred_element_type=F32)
        a_ref[:, cols] = (y * pscale_ref[:, cols]).astype(BF16)

    tail = ext_ref[:, T:T + POOL_HALO, :]
    pstate_ref[...] = tail
    ext_ref[:, 0:POOL_HALO, :] = tail


def _in_proj_prompt_kernel(*refs, T, aliased):
    n_in = 7 + (2 if aliased else 0)
    x_ref, prev_ref, gn_ref, win_ref, wkt_ref, wpool_ref, pscale_ref = refs[:7]
    qm_ref, kt_ref, ktb_ref, v_ref, vb_ref, a_ref, pstate_ref, ext_ref = refs[n_in:]
    t = pl.program_id(1)
    h = _rms(x_ref[...], gn_ref[...]).astype(BF16)
    _store_queries(h, win_ref, qm_ref)
    zkt = lax.dot_general(wkt_ref[...], h, (((1,), (1,)), ((), ())), preferred_element_type=F32)
    kt_ref[...] = zkt
    for c in range(T // ATTN_TILE):
        ktb_ref[c] = zkt[:, c * ATTN_TILE:(c + 1) * ATTN_TILE].astype(BF16)
    zv = jnp.dot(h, win_ref[:, 3 * D_MODEL:4 * D_MODEL], preferred_element_type=F32)
    v_ref[...] = zv
    vb_ref[...] = zv.astype(BF16)
    _pool_branch(t, h, prev_ref, win_ref, wpool_ref, pscale_ref, a_ref, pstate_ref, ext_ref, G=1, T=T, pos0=0)


def _in_proj_prompt(x2d, prev, gn, win, wkt, wpool, pscale, kt_all, v_all, *, layer, depth, n_seq, seq_len):
    N = x2d.shape[0]
    T = ROW_TILE_PROJ
    tiles_per_seq = seq_len // T
    blocks_per_tile = T // ATTN_TILE
    aliased = kt_all is not None
    row = lambda b, t: (b * tiles_per_seq + t, 0)
    out_shape = (
        jax.ShapeDtypeStruct((2, N, D_MODEL), BF16),
        jax.ShapeDtypeStruct((depth, n_seq, D_MODEL, seq_len), F32),
        jax.ShapeDtypeStruct((n_seq, seq_len // ATTN_TILE, D_MODEL, ATTN_TILE), BF16),
        jax.ShapeDtypeStruct((depth, N, D_MODEL), F32),
        jax.ShapeDtypeStruct((N, D_MODEL), BF16),
        jax.ShapeDtypeStruct((N, D_MODEL), BF16),
        jax.ShapeDtypeStruct((n_seq, POOL_HALO, D_MODEL), F32),
    )
    in_specs = [
        pl.BlockSpec((T, D_MODEL), row),
        pl.BlockSpec((1, POOL_HALO, D_MODEL), lambda b, t: (b, 0, 0)),
        _const_spec((1, D_MODEL)),
        _const_spec((D_MODEL, 4 * D_MODEL)),
        _const_spec((D_MODEL, D_MODEL)),
        _const_spec((len(POOL_WINDOWS), POOL_GROUP, POOL_GROUP)),
        _const_spec((1, D_MODEL)),
    ]
    args = [x2d, prev, gn, win, wkt, wpool, pscale]
    aliases = {}
    if aliased:
        in_specs += [pl.BlockSpec(memory_space=pl.ANY), pl.BlockSpec(memory_space=pl.ANY)]
        args += [kt_all, v_all]
        aliases = {7: 1, 8: 3}
    return pl.pallas_call(
        functools.partial(_in_proj_prompt_kernel, T=T, aliased=aliased),
        grid=(n_seq, tiles_per_seq),
        in_specs=in_specs,
        out_specs=(
            pl.BlockSpec((2, T, D_MODEL), lambda b, t: (0, b * tiles_per_seq + t, 0)),
            pl.BlockSpec((None, None, D_MODEL, T), lambda b, t: (layer, b, 0, t)),
            pl.BlockSpec((None, blocks_per_tile, D_MODEL, ATTN_TILE), lambda b, t: (b, t, 0, 0)),
            pl.BlockSpec((None, T, D_MODEL), lambda b, t: (layer, b * tiles_per_seq + t, 0)),
            pl.BlockSpec((T, D_MODEL), row),
            pl.BlockSpec((T, D_MODEL), row),
            pl.BlockSpec((1, POOL_HALO, D_MODEL), lambda b, t: (b, 0, 0)),
        ),
        out_shape=out_shape,
        scratch_shapes=[pltpu.VMEM((1, POOL_HALO + T, D_MODEL), F32)],
        input_output_aliases=aliases,
        compiler_params=pltpu.CompilerParams(
            dimension_semantics=("arbitrary", "arbitrary"), vmem_limit_bytes=VMEM_LIMIT_BYTES),
        name="in_proj_prompt",
    )(*args)


def _in_proj_sample_kernel(x_ref, prev_ref, gn_ref, win_ref, wpool_ref, pscale_ref,
                           qm_ref, k_ref, v_ref, a_ref, pstate_ref, ext_ref, *, G, T, pos0):
    h = _rms(x_ref[...], gn_ref[...]).astype(BF16)
    _store_queries(h, win_ref, qm_ref)
    k_ref[...] = jnp.dot(h, win_ref[:, 2 * D_MODEL:3 * D_MODEL], preferred_element_type=F32)
    v_ref[...] = jnp.dot(h, win_ref[:, 3 * D_MODEL:4 * D_MODEL], preferred_element_type=F32)
    _pool_branch(pl.program_id(0), h, prev_ref, win_ref, wpool_ref, pscale_ref, a_ref, pstate_ref, ext_ref,
                 G=G, T=T, pos0=pos0)


def _in_proj_sample(x2d, prev, gn, win, wpool, pscale, *, n_seq, seq_len, pos0):
    N = x2d.shape[0]
    whole2 = lambda i: (0, 0)
    whole3 = lambda i: (0, 0, 0)
    return pl.pallas_call(
        functools.partial(_in_proj_sample_kernel, G=n_seq, T=seq_len, pos0=pos0),
        grid=(1,),
        in_specs=[
            pl.BlockSpec((N, D_MODEL), whole2),
            pl.BlockSpec((n_seq, POOL_HALO, D_MODEL), whole3),
            _const_spec((1, D_MODEL)),
            _const_spec((D_MODEL, 4 * D_MODEL)),
            _const_spec((len(POOL_WINDOWS), POOL_GROUP, POOL_GROUP)),
            _const_spec((1, D_MODEL)),
        ],
        out_specs=(
            pl.BlockSpec((2, N, D_MODEL), whole3),
            pl.BlockSpec((N, D_MODEL), whole2),
            pl.BlockSpec((N, D_MODEL), whole2),
            pl.BlockSpec((N, D_MODEL), whole2),
            pl.BlockSpec((n_seq, POOL_HALO, D_MODEL), whole3),
        ),
        out_shape=(
            jax.ShapeDtypeStruct((2, N, D_MODEL), BF16),
            jax.ShapeDtypeStruct((N, D_MODEL), F32),
            jax.ShapeDtypeStruct((N, D_MODEL), F32),
            jax.ShapeDtypeStruct((N, D_MODEL), BF16),
            jax.ShapeDtypeStruct((n_seq, POOL_HALO, D_MODEL), F32),
        ),
        scratch_shapes=[pltpu.VMEM((n_seq, POOL_HALO + seq_len, D_MODEL), F32)],
        compiler_params=pltpu.CompilerParams(
            dimension_semantics=("arbitrary",), vmem_limit_bytes=VMEM_LIMIT_BYTES),
        name="in_proj_sample",
    )(x2d, prev, gn, win, wpool, pscale)


def _prompt_attn_kernel(slope_ref, qm_ref, kt_ref, vb_ref, lam_ref, sg_ref, o_ref,
                        s_ref, macc_ref, mb_ref, lacc_ref, r_ref, acc_ref, *, layer, TQ):
    hd = pl.program_id(1)
    qi = pl.program_id(2)
    HALF = TQ // 2
    slope2 = slope_ref[hd] * LOG2E
    q2 = qm_ref[...].reshape(2 * TQ, V_DIM)
    key_iota = lax.broadcasted_iota(jnp.int32, (1, TQ), 1)

    def for_blocks(n, fn):
        def pair(i, c):
            fn(2 * i)
            fn(2 * i + 1)
            return c

        lax.fori_loop(0, n // 2, pair, 0)

        @pl.when(n % 2 == 1)
        def _():
            fn(n - 1)

    def scores(j, masked):
        s = jnp.dot(q2, kt_ref[j], preferred_element_type=F32)
        s = s + slope2 * ((j - qi) * TQ + key_iota).astype(F32)
        if masked:
            qrow = lax.broadcasted_iota(jnp.int32, (2 * TQ, 1), 0) % TQ
            s = jnp.where(key_iota <= qrow, s, NEG_BIG)
        s_ref[j] = s
        macc_ref[...] = jnp.maximum(macc_ref[...], jnp.maximum(s[:, 0:HALF], s[:, HALF:TQ]))

    macc_ref[...] = jnp.full(macc_ref.shape, NEG_BIG, F32)
    for_blocks(qi, lambda j: scores(j, False))
    scores(qi, True)

    m = jnp.max(macc_ref[...], axis=-1, keepdims=True)
    mb_ref[...] = jnp.broadcast_to(m, mb_ref.shape)
    lacc_ref[...] = jnp.zeros(lacc_ref.shape, F32)

    def probs(j):
        mb = mb_ref[...]
        p_lo = jnp.exp2(s_ref[j, :, 0:HALF] - mb)
        p_hi = jnp.exp2(s_ref[j, :, HALF:TQ] - mb)
        s_ref[j, :, 0:HALF] = p_lo
        s_ref[j, :, HALF:TQ] = p_hi
        lacc_ref[...] += p_lo + p_hi

    for_blocks(qi + 1, probs)

    lacc = lacc_ref[...]
    lacc_hi = lacc.astype(BF16)
    lacc_lo = (lacc - lacc_hi.astype(F32)).astype(BF16)
    ones = jnp.ones((HALF, HALF), BF16)
    lsum = (jnp.dot(lacc_hi, ones, preferred_element_type=F32)
            + jnp.dot(lacc_lo, ones, preferred_element_type=F32))
    lam = _diff_lambda(lam_ref, layer)
    r_ref[0:TQ, :] = 1.0 / lsum[0:TQ]
    r_ref[TQ:2 * TQ, :] = lam / lsum[TQ:2 * TQ]
    acc_ref[...] = jnp.zeros(acc_ref.shape, F32)

    def weighted_values(j):
        r0 = r_ref[0:TQ, :]
        r1 = r_ref[TQ:2 * TQ, :]
        a_lo = s_ref[j, 0:TQ, 0:HALF] * r0 - s_ref[j, TQ:2 * TQ, 0:HALF] * r1
        a_hi = s_ref[j, 0:TQ, HALF:TQ] * r0 - s_ref[j, TQ:2 * TQ, HALF:TQ] * r1
        a = jnp.concatenate([a_lo, a_hi], axis=-1).astype(BF16)
        vj = vb_ref[pl.ds(pl.multiple_of(j * TQ, TQ), TQ), :]
        acc_ref[...] += jnp.dot(a, vj, preferred_element_type=F32)

    for_blocks(qi + 1, weighted_values)

    o = _rms(acc_ref[...], sg_ref[...]) * (1.0 - _lambda_init(layer))
    o_ref[...] = o.astype(o_ref.dtype)


def _prompt_attn(slopes, qm, ktb, vb, lam_params, subln_gain, *, layer, n_seq, seq_len):
    N = vb.shape[0]
    TQ = ATTN_TILE
    nq = seq_len // TQ
    return pl.pallas_call(
        functools.partial(_prompt_attn_kernel, layer=layer, TQ=TQ),
        grid=(n_seq, N_HEADS, nq),
        in_specs=[
            pl.BlockSpec(memory_space=pltpu.SMEM),
            pl.BlockSpec((2, TQ, V_DIM), lambda b, h, q: (0, b * nq + q, h)),
            pl.BlockSpec((None, nq, V_DIM, TQ), lambda b, h, q: (b, 0, h, 0)),
            pl.BlockSpec((seq_len, V_DIM), lambda b, h, q: (b, h)),
            _const_spec((4, HEAD_DIM)),
            _const_spec((1, V_DIM)),
        ],
        out_specs=pl.BlockSpec((TQ, V_DIM), lambda b, h, q: (b * nq + q, h)),
        out_shape=jax.ShapeDtypeStruct((N, D_MODEL), BF16),
        scratch_shapes=[
            pltpu.VMEM((nq, 2 * TQ, TQ), F32),
            pltpu.VMEM((2 * TQ, TQ // 2), F32),
            pltpu.VMEM((2 * TQ, TQ // 2), F32),
            pltpu.VMEM((2 * TQ, TQ // 2), F32),
            pltpu.VMEM((2 * TQ, TQ // 2), F32),
            pltpu.VMEM((TQ, V_DIM), F32),
        ],
        compiler_params=pltpu.CompilerParams(
            dimension_semantics=("arbitrary", "arbitrary", "arbitrary"), vmem_limit_bytes=VMEM_LIMIT_BYTES),
        name="prompt_attn",
    )(slopes, qm, ktb, vb, lam_params, subln_gain)


def _sample_attn_kernel(pt_ref, qbd_ref, rslope_ref, rqpos_ref, knew_ref, vnew_ref, lam_ref, sg_ref, *rest,
                        layer, n_pages_step, past_len, L):
    P = n_pages_step
    kpages = rest[:P]
    vpages = rest[P:2 * P]
    o_ref = rest[2 * P]
    m_ref, l_ref, acc_ref, knew_pad_ref, vnew_pad_ref = rest[2 * P + 1:]
    j = pl.program_id(1)
    RH = 2 * L
    qbd = qbd_ref[...]
    rslope2 = rslope_ref[...] * LOG2E
    rqpos = rqpos_ref[...]

    @pl.when(j == 0)
    def _():
        m_ref[...] = jnp.full(m_ref.shape, NEG_BIG, F32)
        l_ref[...] = jnp.zeros(l_ref.shape, F32)
        acc_ref[...] = jnp.zeros(acc_ref.shape, F32)

    def update(s, kpos, causal, head_values):
        dist = (rqpos - kpos).astype(F32)
        s = s - rslope2 * dist
        if causal:
            s = jnp.where(dist >= 0.0, s, NEG_BIG)
        m_old = m_ref[...]
        m_new = jnp.maximum(m_old, jnp.max(s, axis=-1, keepdims=True))
        alpha = jnp.exp2(m_old - m_new)
        p = jnp.exp2(s - m_new)
        l_ref[...] = alpha * l_ref[...] + jnp.sum(p, axis=-1, keepdims=True)
        m_ref[...] = m_new
        pb = p.astype(BF16)
        for hd in range(N_HEADS):
            rows = slice(hd * RH, (hd + 1) * RH)
            pv = jnp.dot(pb[rows, :], head_values(hd), preferred_element_type=F32)
            acc_ref[rows, :] = alpha[rows, :] * acc_ref[rows, :] + pv

    s_pages = [jnp.dot(qbd, kpages[p_i][...].astype(BF16), preferred_element_type=F32) for p_i in range(P)]
    kpos = j * (P * PAGE_SIZE) + lax.broadcasted_iota(jnp.int32, (1, P * PAGE_SIZE), 1)

    def cache_values(hd):
        rows = [vpages[p_i][pl.ds(hd, PAGE_SIZE, stride=N_HEADS), :] for p_i in range(P)]
        return jnp.concatenate(rows, axis=0).astype(BF16)

    update(jnp.concatenate(s_pages, axis=1), kpos, False, cache_values)

    @pl.when(j == pl.num_programs(1) - 1)
    def _():
        knew_pad_ref[...] = jnp.zeros(knew_pad_ref.shape, F32)
        vnew_pad_ref[...] = jnp.zeros(vnew_pad_ref.shape, F32)
        knew_pad_ref[0:L, :] = knew_ref[...]
        vnew_pad_ref[0:L, :] = vnew_ref[...]
        s_new = lax.dot_general(qbd, knew_pad_ref[...].astype(BF16), (((1,), (1,)), ((), ())),
                                preferred_element_type=F32)
        kpos_new = past_len + lax.broadcasted_iota(jnp.int32, (1, PAGE_SIZE), 1)
        update(s_new, kpos_new, True, lambda hd: vnew_pad_ref[:, hd * V_DIM:(hd + 1) * V_DIM].astype(BF16))
        lam = _diff_lambda(lam_ref, layer)
        on = acc_ref[...] / l_ref[...]
        for hd in range(N_HEADS):
            r0 = hd * RH
            o = on[r0:r0 + L, :] - lam * on[r0 + L:r0 + 2 * L, :]
            o = _rms(o, sg_ref[...]) * (1.0 - _lambda_init(layer))
            o_ref[:, hd * V_DIM:(hd + 1) * V_DIM] = o.astype(o_ref.dtype)


def _sample_attn(page_table, qbd, rslope, rqpos, k_new, v_new, lam_params, subln_gain, cache_kt, cache_vr,
                 *, layer, n_seq, L):
    n_pages = page_table.shape[1]
    P = PAGES_PER_STEP
    n_steps = n_pages // P
    R = N_HEADS * 2 * L

    def page_spec(p_i):
        return pl.BlockSpec((None, None, PAGE_SIZE * N_HEADS, V_DIM),
                            lambda b, j, pt: (layer, pt[b, j * P + p_i], 0, 0))

    grid_spec = pltpu.PrefetchScalarGridSpec(
        num_scalar_prefetch=1,
        grid=(n_seq, n_steps),
        in_specs=[
            pl.BlockSpec((None, R, D_MODEL), lambda b, j, pt: (b, 0, 0)),
            pl.BlockSpec((R, 1), lambda b, j, pt: (0, 0)),
            pl.BlockSpec((R, 1), lambda b, j, pt: (0, 0)),
            pl.BlockSpec((L, D_MODEL), lambda b, j, pt: (b, 0)),
            pl.BlockSpec((L, D_MODEL), lambda b, j, pt: (b, 0)),
            pl.BlockSpec((4, HEAD_DIM), lambda b, j, pt: (0, 0)),
            pl.BlockSpec((1, V_DIM), lambda b, j, pt: (0, 0)),
        ] + [page_spec(p_i) for p_i in range(P)] + [page_spec(p_i) for p_i in range(P)],
        out_specs=pl.BlockSpec((L, D_MODEL), lambda b, j, pt: (b, 0)),
        scratch_shapes=[pltpu.VMEM((R, 1), F32), pltpu.VMEM((R, 1), F32), pltpu.VMEM((R, V_DIM), F32),
                        pltpu.VMEM((PAGE_SIZE, D_MODEL), F32), pltpu.VMEM((PAGE_SIZE, D_MODEL), F32)],
    )
    return pl.pallas_call(
        functools.partial(_sample_attn_kernel, layer=layer, n_pages_step=P, past_len=n_pages * PAGE_SIZE, L=L),
        grid_spec=grid_spec,
        out_shape=jax.ShapeDtypeStruct((n_seq * L, D_MODEL), F32),
        compiler_params=pltpu.CompilerParams(
            dimension_semantics=("arbitrary", "arbitrary"), vmem_limit_bytes=VMEM_LIMIT_BYTES),
        name="sample_attn",
    )(page_table, qbd, rslope, rqpos, k_new, v_new, lam_params, subln_gain,
      *([cache_kt] * P), *([cache_vr] * P))


def _mix_kernel(x_ref, a_ref, o_ref, gn_ref, wg_ref, bg_ref, wbp_ref, wba_ref, wout_ref, gpost_ref, y_ref):
    x = x_ref[...]
    h = _rms(x, gn_ref[...]).astype(BF16)
    g = jax.nn.sigmoid(jnp.dot(h, wg_ref[...], preferred_element_type=F32) + bg_ref[...])
    bp = jnp.dot(a_ref[...].astype(BF16), wbp_ref[...], preferred_element_type=F32)
    ba = jnp.dot(o_ref[...].astype(BF16), wba_ref[...], preferred_element_type=F32)
    mix = g[:, 0:D_MODEL] * bp + g[:, D_MODEL:2 * D_MODEL] * ba
    y = jnp.dot(mix.astype(BF16), wout_ref[...], preferred_element_type=F32)
    y_ref[...] = x + _rms(y, gpost_ref[...])


def _mix(x2d, a, o, gn, wg, bg, wbp, wba, wout, gpost, *, tile):
    N = x2d.shape[0]
    row = lambda i: (i, 0)
    return pl.pallas_call(
        _mix_kernel,
        grid=(N // tile,),
        in_specs=[
            pl.BlockSpec((tile, D_MODEL), row),
            pl.BlockSpec((tile, D_MODEL), row),
            pl.BlockSpec((tile, D_MODEL), row),
            _const_spec((1, D_MODEL)),
            _const_spec((D_MODEL, 2 * D_MODEL)),
            _const_spec((1, 2 * D_MODEL)),
            _const_spec((D_MODEL, D_MODEL)),
            _const_spec((D_MODEL, D_MODEL)),
            _const_spec((D_MODEL, D_MODEL)),
            _const_spec((1, D_MODEL)),
        ],
        out_specs=pl.BlockSpec((tile, D_MODEL), row),
        out_shape=jax.ShapeDtypeStruct((N, D_MODEL), F32),
        compiler_params=pltpu.CompilerParams(
            dimension_semantics=("arbitrary",), vmem_limit_bytes=VMEM_LIMIT_BYTES),
        name="mix",
    )(x2d, a, o, gn, wg, bg, wbp, wba, wout, gpost)


def _ffn_kernel(x_ref, prev_ref, gn_ref, wup_ref, cw_ref, cb_ref, wdown_ref, gpost_ref,
                y_ref, fstate_ref, ext_ref, *, G, T):
    t = pl.program_id(1)
    M = G * T
    x = x_ref[...]
    hn = _rms(x, gn_ref[...]).astype(BF16)

    @pl.when(t == 0)
    def _():
        ext_ref[:, 0:CONV_HALO, :] = prev_ref[...]

    gpre = jnp.dot(hn, wup_ref[:, 0:D_FF], preferred_element_type=F32)
    ext_ref[:, CONV_HALO:CONV_HALO + T, :] = gpre.reshape(G, T, D_FF)
    val = jnp.dot(hn, wup_ref[:, D_FF:2 * D_FF], preferred_element_type=F32)

    cw = cw_ref[...]
    c = cb_ref[...].reshape(1, 1, D_FF)
    for j in range(CONV_W):
        off = CONV_HALO - (CONV_W - 1) + j
        c = c + cw[j:j + 1, :].reshape(1, 1, D_FF) * ext_ref[:, off:off + T, :]
    act = jax.nn.gelu(c, approximate=True).reshape(M, D_FF) * val
    f = jnp.dot(act.astype(BF16), wdown_ref[...], preferred_element_type=F32)
    y_ref[...] = x + _rms(f, gpost_ref[...])

    tail = ext_ref[:, T:T + CONV_HALO, :]
    fstate_ref[...] = tail
    ext_ref[:, 0:CONV_HALO, :] = tail


def _ffn(x2d, prev, gn, wup, cw, cb, wdown, gpost, *, n_seq, seq_len, G, T):
    N = x2d.shape[0]
    M = G * T
    tiles_per_seq = seq_len // T
    n_outer = n_seq // G
    row = lambda b, t: (b * tiles_per_seq + t, 0)
    return pl.pallas_call(
        functools.partial(_ffn_kernel, G=G, T=T),
        grid=(n_outer, tiles_per_seq),
        in_specs=[
            pl.BlockSpec((M, D_MODEL), row),
            pl.BlockSpec((G, CONV_HALO, D_FF), lambda b, t: (b, 0, 0)),
            _const_spec((1, D_MODEL)),
            _const_spec((D_MODEL, 2 * D_FF)),
            _const_spec((CONV_W, D_FF)),
            _const_spec((1, D_FF)),
            _const_spec((D_FF, D_MODEL)),
            _const_spec((1, D_MODEL)),
        ],
        out_specs=(
            pl.BlockSpec((M, D_MODEL), row),
            pl.BlockSpec((G, CONV_HALO, D_FF), lambda b, t: (b, 0, 0)),
        ),
        out_shape=(
            jax.ShapeDtypeStruct((N, D_MODEL), F32),
            jax.ShapeDtypeStruct((n_seq, CONV_HALO, D_FF), F32),
        ),
        scratch_shapes=[pltpu.VMEM((G, CONV_HALO + T, D_FF), F32)],
        compiler_params=pltpu.CompilerParams(
            dimension_semantics=("arbitrary", "arbitrary"), vmem_limit_bytes=VMEM_LIMIT_BYTES),
        name="ffn",
    )(x2d, prev, gn, wup, cw, cb, wdown, gpost)


def _block_diag_queries(qm, n_seq, L):
    q = qm.reshape(2, n_seq, L, N_HEADS, V_DIM)
    eye = jnp.eye(N_HEADS, dtype=qm.dtype)
    qbd = jnp.einsum('cbihd,hg->bgcihd', q, eye)
    return qbd.reshape(n_seq, N_HEADS * 2 * L, D_MODEL)


def kernel(x_prompt, x_sample, cache_k, cache_v, state_pool, state_ffn, page_table, norm_pre_mix, w_in,
           w_pool_group, pool_scale, lambda_q1, lambda_k1, lambda_q2, lambda_k2, subln_gain, w_gate, b_gate,
           w_branch_pool, w_branch_attn, w_out, norm_post_mix, norm_pre_ffn, w_up, conv_w, conv_b, w_down,
           norm_post_ffn):
    depth = w_in.shape[0]
    n_p, s_p, _ = x_prompt.shape
    n_s, s_s, _ = x_sample.shape
    n_pages = page_table.shape[1]
    past_len = n_pages * PAGE_SIZE
    n_pool = cache_k.shape[1]

    cache_kt = jnp.transpose(cache_k, (0, 1, 3, 4, 5, 2)).reshape(depth, n_pool, D_MODEL, PAGE_SIZE)
    cache_vr = cache_v.reshape(depth, n_pool, PAGE_SIZE * N_HEADS, V_DIM)
    slopes = jnp.exp2(-8.0 * jnp.arange(1, N_HEADS + 1, dtype=F32) / N_HEADS)
    rslope = jnp.repeat(slopes, 2 * s_s).reshape(N_HEADS * 2 * s_s, 1)
    rqpos = jnp.tile(past_len + jnp.arange(s_s, dtype=jnp.int32), N_HEADS * 2).reshape(N_HEADS * 2 * s_s, 1)

    pool_prev_p = jnp.zeros((n_p, POOL_HALO, D_MODEL), F32)
    ffn_prev_p = jnp.zeros((n_p, CONV_HALO, D_FF), F32)
    pool_prev_s = jnp.pad(state_pool, ((0, 0), (0, 0), (POOL_HALO - POOL_PAD, 0), (0, 0)))
    ffn_prev_s = jnp.pad(state_ffn, ((0, 0), (0, 0), (CONV_HALO - (CONV_W - 1), 0), (0, 0)))

    xp = x_prompt.reshape(n_p * s_p, D_MODEL)
    xs = x_sample.reshape(n_s * s_s, D_MODEL)
    outs = {name: [] for name in ("pp", "fp", "ks", "vs", "ps", "fs")}
    kt_all = v_all = None

    for l in range(depth):
        gn = norm_pre_mix[l].reshape(1, D_MODEL)
        win = w_in[l].astype(BF16)
        wkt = w_in[l, :, 2 * D_MODEL:3 * D_MODEL].T.astype(BF16)
        wpool = w_pool_group[l].astype(BF16)
        pscale = pool_scale[l].reshape(1, D_MODEL)
        lam_params = jnp.stack([lambda_q1[l], lambda_k1[l], lambda_q2[l], lambda_k2[l]])
        sg = subln_gain[l].reshape(1, V_DIM)
        mix_w = (gn, w_gate[l].astype(BF16), b_gate[l].reshape(1, 2 * D_MODEL), w_branch_pool[l].astype(BF16),
                 w_branch_attn[l].astype(BF16), w_out[l].astype(BF16), norm_post_mix[l].reshape(1, D_MODEL))
        ffn_w = (norm_pre_ffn[l].reshape(1, D_MODEL), w_up[l].astype(BF16), conv_w[l], conv_b[l].reshape(1, D_FF),
                 w_down[l].astype(BF16), norm_post_ffn[l].reshape(1, D_MODEL))

        qm, kt_all, ktb, v_all, vb, a, pstate = _in_proj_prompt(
            xp, pool_prev_p, gn, win, wkt, wpool, pscale, kt_all, v_all,
            layer=l, depth=depth, n_seq=n_p, seq_len=s_p)
        o = _prompt_attn(slopes, qm, ktb, vb, lam_params, sg, layer=l, n_seq=n_p, seq_len=s_p)
        x1 = _mix(xp, a, o, *mix_w, tile=ROW_TILE_MIX)
        xp, fstate = _ffn(x1, ffn_prev_p, *ffn_w, n_seq=n_p, seq_len=s_p, G=1, T=ROW_TILE_FFN)
        outs["pp"].append(pstate[:, POOL_HALO - POOL_PAD:, :])
        outs["fp"].append(fstate[:, CONV_HALO - (CONV_W - 1):, :])

        qm, k, v, a, pstate = _in_proj_sample(xs, pool_prev_s[l], gn, win, wpool, pscale,
                                              n_seq=n_s, seq_len=s_s, pos0=past_len)
        qbd = _block_diag_queries(qm, n_s, s_s)
        o = _sample_attn(page_table, qbd, rslope, rqpos, k, v, lam_params, sg, cache_kt, cache_vr,
                         layer=l, n_seq=n_s, L=s_s)
        x1 = _mix(xs, a, o, *mix_w, tile=n_s * s_s)
        xs, fstate = _ffn(x1, ffn_prev_s[l], *ffn_w, n_seq=n_s, seq_len=s_s, G=n_s, T=s_s)
        outs["ks"].append(k.reshape(n_s, s_s, N_HEADS, 2, HEAD_DIM))
        outs["vs"].append(v.reshape(n_s, s_s, N_HEADS, V_DIM))
        outs["ps"].append(pstate[:, POOL_HALO - POOL_PAD:, :])
        outs["fs"].append(fstate[:, CONV_HALO - (CONV_W - 1):, :])

    st = {name: jnp.stack(v) for name, v in outs.items()}
    k_prompt = jnp.transpose(kt_all.reshape(depth, n_p, N_HEADS, 2, HEAD_DIM, s_p), (0, 1, 5, 2, 3, 4))
    v_prompt = v_all.reshape(depth, n_p, s_p, N_HEADS, V_DIM)
    return (xp.reshape(n_p, s_p, D_MODEL), xs.reshape(n_s, s_s, D_MODEL),
            k_prompt, v_prompt, st["pp"], st["fp"], st["ks"], st["vs"], st["ps"], st["fs"])
```

```python
import functools
import math

import jax
import jax.numpy as jnp
from jax import lax
from jax.experimental import pallas as pl
from jax.experimental.pallas import tpu as pltpu

F32 = jnp.float32
BF16 = jnp.bfloat16

D_MODEL = 1024
N_HEADS = 8
HEAD_DIM = 64
V_DIM = 2 * HEAD_DIM
ATTN_SCALE = HEAD_DIM ** -0.5
LOG2E = math.log2(math.e)
POOL_WINDOWS = (2, 4, 8, 16)
POOL_GROUP = D_MODEL // len(POOL_WINDOWS)
POOL_PAD = max(POOL_WINDOWS) - 1
POOL_HALO = 16
D_FF = 2816
CONV_W = 3
CONV_HALO = 8
NORM_EPS = 1e-6
PAGE_SIZE = 128
NEG_BIG = -1e30

VMEM_LIMIT_BYTES = 56 * 1024 * 1024

ROW_TILE_PROJ = 512
ROW_TILE_MIX = 512
ROW_TILE_FFN = 256
ATTN_TILE = 256
PAGES_PER_STEP = 8


def _lambda_init(layer):
    return 0.8 - 0.6 * math.exp(-0.3 * layer)


def _rms(x, g):
    ms = jnp.mean(x * x, axis=-1, keepdims=True)
    return x * lax.rsqrt(ms + NORM_EPS) * g


def _const_spec(shape):
    nd = len(shape)
    return pl.BlockSpec(shape, lambda *_: (0,) * nd, pipeline_mode=pl.Buffered(1))


def _diff_lambda(lam_ref, layer):
    lp = lam_ref[...]
    a = jnp.sum(lp[0:1, :] * lp[1:2, :], axis=-1, keepdims=True)
    b = jnp.sum(lp[2:3, :] * lp[3:4, :], axis=-1, keepdims=True)
    return jnp.exp(a) - jnp.exp(b) + _lambda_init(layer)


def _store_queries(h, win_ref, qm_ref):
    zq = jnp.dot(h, win_ref[:, D_MODEL:2 * D_MODEL], preferred_element_type=F32) * (ATTN_SCALE * LOG2E)
    lane = lax.broadcasted_iota(jnp.int32, (1, D_MODEL), 1)
    first_map = (lane % V_DIM) < HEAD_DIM
    qm_ref[0] = jnp.where(first_map, zq, 0.0).astype(BF16)
    qm_ref[1] = jnp.where(first_map, 0.0, zq).astype(BF16)


def _pool_branch(t, h, prev_ref, win_ref, wpool_ref, pscale_ref, a_ref, pstate_ref, ext_ref, *, G, T, pos0):
    M = G * T

    @pl.when(t == 0)
    def _():
        ext_ref[:, 0:POOL_HALO, :] = prev_ref[...]

    zu = jnp.dot(h, win_ref[:, 0:D_MODEL], preferred_element_type=F32)
    ext_ref[:, POOL_HALO:POOL_HALO + T, :] = zu.reshape(G, T, D_MODEL)

    pos = pos0 + t * T + lax.broadcasted_iota(jnp.int32, (1, T, 1), 1)
    for g, w in enumerate(POOL_WINDOWS):
        cols = slice(g * POOL_GROUP, (g + 1) * POOL_GROUP)
        u_new = ext_ref[:, POOL_HALO:POOL_HALO + T, cols]
        win = u_new
        for j in range(1, w):
            win = win + ext_ref[:, POOL_HALO - j:POOL_HALO - j + T, cols]
        cnt = jnp.minimum(pos + 1, w).astype(F32)
        d = (win / cnt - u_new).astype(BF16).reshape(M, POOL_GROUP)
        y = jnp.dot(d, wpool_ref[g], preferred_element_type=F32)
        a_ref[:, cols] = (y * pscale_ref[:, cols]).astype(BF16)

    tail = ext_ref[:, T:T + POOL_HALO, :]
    pstate_ref[...] = tail
    ext_ref[:, 0:POOL_HALO, :] = tail


def _in_proj_prompt_kernel(*refs, T, aliased):
    n_in = 7 + (2 if aliased else 0)
    x_ref, prev_ref, gn_ref, win_ref, wkt_ref, wpool_ref, pscale_ref = refs[:7]
    qm_ref, kt_ref, ktb_ref, v_ref, vb_ref, a_ref, pstate_ref, ext_ref = refs[n_in:]
    t = pl.program_id(1)
    h = _rms(x_ref[...], gn_ref[...]).astype(BF16)
    _store_queries(h, win_ref, qm_ref)
    zkt = lax.dot_general(wkt_ref[...], h, (((1,), (1,)), ((), ())), preferred_element_type=F32)
    kt_ref[...] = zkt
    for c in range(T // ATTN_TILE):
        ktb_ref[c] = zkt[:, c * ATTN_TILE:(c + 1) * ATTN_TILE].astype(BF16)
    zv = jnp.dot(h, win_ref[:, 3 * D_MODEL:4 * D_MODEL], preferred_element_type=F32)
    v_ref[...] = zv
    vb_ref[...] = zv.astype(BF16)
    _pool_branch(t, h, prev_ref, win_ref, wpool_ref, pscale_ref, a_ref, pstate_ref, ext_ref, G=1, T=T, pos0=0)


def _in_proj_prompt(x2d, prev, gn, win, wkt, wpool, pscale, kt_all, v_all, *, layer, depth, n_seq, seq_len):
    N = x2d.shape[0]
    T = ROW_TILE_PROJ
    tiles_per_seq = seq_len // T
    blocks_per_tile = T // ATTN_TILE
    aliased = kt_all is not None
    row = lambda b, t: (b * tiles_per_seq + t, 0)
    out_shape = (
        jax.ShapeDtypeStruct((2, N, D_MODEL), BF16),
        jax.ShapeDtypeStruct((depth, n_seq, D_MODEL, seq_len), F32),
        jax.ShapeDtypeStruct((n_seq, seq_len // ATTN_TILE, D_MODEL, ATTN_TILE), BF16),
        jax.ShapeDtypeStruct((depth, N, D_MODEL), F32),
        jax.ShapeDtypeStruct((N, D_MODEL), BF16),
        jax.ShapeDtypeStruct((N, D_MODEL), BF16),
        jax.ShapeDtypeStruct((n_seq, POOL_HALO, D_MODEL), F32),
    )
    in_specs = [
        pl.BlockSpec((T, D_MODEL), row),
        pl.BlockSpec((1, POOL_HALO, D_MODEL), lambda b, t: (b, 0, 0)),
        _const_spec((1, D_MODEL)),
        _const_spec((D_MODEL, 4 * D_MODEL)),
        _const_spec((D_MODEL, D_MODEL)),
        _const_spec((len(POOL_WINDOWS), POOL_GROUP, POOL_GROUP)),
        _const_spec((1, D_MODEL)),
    ]
    args = [x2d, prev, gn, win, wkt, wpool, pscale]
    aliases = {}
    if aliased:
        in_specs += [pl.BlockSpec(memory_space=pl.ANY), pl.BlockSpec(memory_space=pl.ANY)]
        args += [kt_all, v_all]
        aliases = {7: 1, 8: 3}
    return pl.pallas_call(
        functools.partial(_in_proj_prompt_kernel, T=T, aliased=aliased),
        grid=(n_seq, tiles_per_seq),
        in_specs=in_specs,
        out_specs=(
            pl.BlockSpec((2, T, D_MODEL), lambda b, t: (0, b * tiles_per_seq + t, 0)),
            pl.BlockSpec((None, None, D_MODEL, T), lambda b, t: (layer, b, 0, t)),
            pl.BlockSpec((None, blocks_per_tile, D_MODEL, ATTN_TILE), lambda b, t: (b, t, 0, 0)),
            pl.BlockSpec((None, T, D_MODEL), lambda b, t: (layer, b * tiles_per_seq + t, 0)),
            pl.BlockSpec((T, D_MODEL), row),
            pl.BlockSpec((T, D_MODEL), row),
            pl.BlockSpec((1, POOL_HALO, D_MODEL), lambda b, t: (b, 0, 0)),
        ),
        out_shape=out_shape,
        scratch_shapes=[pltpu.VMEM((1, POOL_HALO + T, D_MODEL), F32)],
        input_output_aliases=aliases,
        compiler_params=pltpu.CompilerParams(
            dimension_semantics=("arbitrary", "arbitrary"), vmem_limit_bytes=VMEM_LIMIT_BYTES),
        name="in_proj_prompt",
    )(*args)


def _in_proj_sample_kernel(x_ref, prev_ref, gn_ref, win_ref, wpool_ref, pscale_ref,
                           qm_ref, k_ref, v_ref, a_ref, pstate_ref, ext_ref, *, G, T, pos0):
    h = _rms(x_ref[...], gn_ref[...]).astype(BF16)
    _store_queries(h, win_ref, qm_ref)
    k_ref[...] = jnp.dot(h, win_ref[:, 2 * D_MODEL:3 * D_MODEL], preferred_element_type=F32)
    v_ref[...] = jnp.dot(h, win_ref[:, 3 * D_MODEL:4 * D_MODEL], preferred_element_type=F32)
    _pool_branch(pl.program_id(0), h, prev_ref, win_ref, wpool_ref, pscale_ref, a_ref, pstate_ref, ext_ref,
                 G=G, T=T, pos0=pos0)


def _in_proj_sample(x2d, prev, gn, win, wpool, pscale, *, n_seq, seq_len, pos0):
    N = x2d.shape[0]
    whole2 = lambda i: (0, 0)
    whole3 = lambda i: (0, 0, 0)
    return pl.pallas_call(
        functools.partial(_in_proj_sample_kernel, G=n_seq, T=seq_len, pos0=pos0),
        grid=(1,),
        in_specs=[
            pl.BlockSpec((N, D_MODEL), whole2),
            pl.BlockSpec((n_seq, POOL_HALO, D_MODEL), whole3),
            _const_spec((1, D_MODEL)),
            _const_spec((D_MODEL, 4 * D_MODEL)),
            _const_spec((len(POOL_WINDOWS), POOL_GROUP, POOL_GROUP)),
            _const_spec((1, D_MODEL)),
        ],
        out_specs=(
            pl.BlockSpec((2, N, D_MODEL), whole3),
            pl.BlockSpec((N, D_MODEL), whole2),
            pl.BlockSpec((N, D_MODEL), whole2),
            pl.BlockSpec((N, D_MODEL), whole2),
            pl.BlockSpec((n_seq, POOL_HALO, D_MODEL), whole3),
        ),
        out_shape=(
            jax.ShapeDtypeStruct((2, N, D_MODEL), BF16),
            jax.ShapeDtypeStruct((N, D_MODEL), F32),
            jax.ShapeDtypeStruct((N, D_MODEL), F32),
            jax.ShapeDtypeStruct((N, D_MODEL), BF16),
            jax.ShapeDtypeStruct((n_seq, POOL_HALO, D_MODEL), F32),
        ),
        scratch_shapes=[pltpu.VMEM((n_seq, POOL_HALO + seq_len, D_MODEL), F32)],
        compiler_params=pltpu.CompilerParams(
            dimension_semantics=("arbitrary",), vmem_limit_bytes=VMEM_LIMIT_BYTES),
        name="in_proj_sample",
    )(x2d, prev, gn, win, wpool, pscale)


def _prompt_attn_kernel(slope_ref, qm_ref, kt_ref, vb_ref, lam_ref, sg_ref, o_ref, *scratch, layer, TQ, nq):
    s_ref, macc_ref, mb_ref, lacc_ref, r_ref, acc_ref = (scratch[3 * i:3 * i + 3] for i in range(6))
    hd = pl.program_id(1)
    HALF = TQ // 2
    slope2 = slope_ref[hd] * LOG2E
    key_iota = lax.broadcasted_iota(jnp.int32, (1, TQ), 1)

    def scores(T, slot, j, masked):
        q2 = qm_ref[:, T * TQ:(T + 1) * TQ, :].reshape(2 * TQ, V_DIM)
        s = jnp.dot(q2, kt_ref[j], preferred_element_type=F32)
        s = s + slope2 * ((j - T) * TQ + key_iota).astype(F32)
        if masked:
            qrow = lax.broadcasted_iota(jnp.int32, (2 * TQ, 1), 0) % TQ
            s = jnp.where(key_iota <= qrow, s, NEG_BIG)
        s_ref[slot][j] = s
        macc_ref[slot][...] = jnp.maximum(macc_ref[slot][...], jnp.maximum(s[:, 0:HALF], s[:, HALF:TQ]))

    def probs(slot, j):
        mb = mb_ref[slot][...]
        p_lo = jnp.exp2(s_ref[slot][j, :, 0:HALF] - mb)
        p_hi = jnp.exp2(s_ref[slot][j, :, HALF:TQ] - mb)
        s_ref[slot][j, :, 0:HALF] = p_lo
        s_ref[slot][j, :, HALF:TQ] = p_hi
        lacc_ref[slot][...] += p_lo + p_hi

    def weighted_values(slot, j):
        r0 = r_ref[slot][0:TQ, :]
        r1 = r_ref[slot][TQ:2 * TQ, :]
        a_lo = s_ref[slot][j, 0:TQ, 0:HALF] * r0 - s_ref[slot][j, TQ:2 * TQ, 0:HALF] * r1
        a_hi = s_ref[slot][j, 0:TQ, HALF:TQ] * r0 - s_ref[slot][j, TQ:2 * TQ, HALF:TQ] * r1
        a = jnp.concatenate([a_lo, a_hi], axis=-1).astype(BF16)
        vj = vb_ref[pl.ds(j * TQ if isinstance(j, int) else pl.multiple_of(j * TQ, TQ), TQ), :]
        acc_ref[slot][...] += jnp.dot(a, vj, preferred_element_type=F32)

    def stage(T, do_s, do_p, do_v):
        slot_s, slot_p, slot_v = T % 3, (T + 2) % 3, (T + 1) % 3
        if do_s:
            macc_ref[slot_s][...] = jnp.full(macc_ref[slot_s].shape, NEG_BIG, F32)
        if do_p:
            lacc_ref[slot_p][...] = jnp.zeros(lacc_ref[slot_p].shape, F32)
        if do_v:
            acc_ref[slot_v][...] = jnp.zeros(acc_ref[slot_v].shape, F32)

        def fused(j, c):
            if do_s:
                scores(T, slot_s, j, False)
            if do_p:
                probs(slot_p, j)
            if do_v:
                weighted_values(slot_v, j)
            return c

        if do_v:
            lax.fori_loop(0, T - 1, fused, 0)
        if do_p:
            if do_s:
                scores(T, slot_s, T - 1, False)
            probs(slot_p, T - 1)
        if do_s:
            scores(T, slot_s, T, True)
            m = jnp.max(macc_ref[slot_s][...], axis=-1, keepdims=True)
            mb_ref[slot_s][...] = jnp.broadcast_to(m, mb_ref[slot_s].shape)
        if do_p:
            ones = jnp.ones((HALF, HALF), BF16)
            lsum = jnp.dot(lacc_ref[slot_p][...].astype(BF16), ones, preferred_element_type=F32)
            r_ref[slot_p][0:TQ, :] = 1.0 / lsum[0:TQ]
            r_ref[slot_p][TQ:2 * TQ, :] = _diff_lambda(lam_ref, layer) / lsum[TQ:2 * TQ]
        if do_v:
            o = _rms(acc_ref[slot_v][...], sg_ref[...]) * (1.0 - _lambda_init(layer))
            o_ref[(T - 2) * TQ:(T - 1) * TQ, :] = o.astype(o_ref.dtype)

    for T in range(nq + 2):
        stage(T, T < nq, 1 <= T <= nq, T >= 2)


def _prompt_attn(slopes, qm, ktb, vb, lam_params, subln_gain, *, layer, n_seq, seq_len):
    N = vb.shape[0]
    TQ = ATTN_TILE
    nq = seq_len // TQ
    assert nq >= 2
    return pl.pallas_call(
        functools.partial(_prompt_attn_kernel, layer=layer, TQ=TQ, nq=nq),
        grid=(n_seq, N_HEADS),
        in_specs=[
            pl.BlockSpec(memory_space=pltpu.SMEM),
            pl.BlockSpec((2, seq_len, V_DIM), lambda b, h: (0, b, h)),
            pl.BlockSpec((None, nq, V_DIM, TQ), lambda b, h: (b, 0, h, 0)),
            pl.BlockSpec((seq_len, V_DIM), lambda b, h: (b, h)),
            _const_spec((4, HEAD_DIM)),
            _const_spec((1, V_DIM)),
        ],
        out_specs=pl.BlockSpec((seq_len, V_DIM), lambda b, h: (b, h)),
        out_shape=jax.ShapeDtypeStruct((N, D_MODEL), BF16),
        scratch_shapes=(
            [pltpu.VMEM((nq, 2 * TQ, TQ), F32)] * 3
            + [pltpu.VMEM((2 * TQ, TQ // 2), F32)] * 12
            + [pltpu.VMEM((TQ, V_DIM), F32)] * 3
        ),
        compiler_params=pltpu.CompilerParams(
            dimension_semantics=("arbitrary", "arbitrary"), vmem_limit_bytes=VMEM_LIMIT_BYTES),
        name="prompt_attn",
    )(slopes, qm, ktb, vb, lam_params, subln_gain)


def _sample_attn_kernel(pt_ref, qbd_ref, rslope_ref, rqpos_ref, knew_ref, vnew_ref, lam_ref, sg_ref, *rest,
                        layer, n_pages_step, past_len, L):
    P = n_pages_step
    kpages = rest[:P]
    vpages = rest[P:2 * P]
    o_ref = rest[2 * P]
    m_ref, l_ref, acc_ref, knew_pad_ref, vnew_pad_ref = rest[2 * P + 1:]
    j = pl.program_id(1)
    RH = 2 * L
    qbd = qbd_ref[...]
    rslope2 = rslope_ref[...] * LOG2E
    rqpos = rqpos_ref[...]

    @pl.when(j == 0)
    def _():
        m_ref[...] = jnp.full(m_ref.shape, NEG_BIG, F32)
        l_ref[...] = jnp.zeros(l_ref.shape, F32)
        acc_ref[...] = jnp.zeros(acc_ref.shape, F32)

    def update(s, kpos, causal, head_values):
        dist = (rqpos - kpos).astype(F32)
        s = s - rslope2 * dist
        if causal:
            s = jnp.where(dist >= 0.0, s, NEG_BIG)
        m_old = m_ref[...]
        m_new = jnp.maximum(m_old, jnp.max(s, axis=-1, keepdims=True))
        alpha = jnp.exp2(m_old - m_new)
        p = jnp.exp2(s - m_new)
        l_ref[...] = alpha * l_ref[...] + jnp.sum(p, axis=-1, keepdims=True)
        m_ref[...] = m_new
        pb = p.astype(BF16)
        for hd in range(N_HEADS):
            rows = slice(hd * RH, (hd + 1) * RH)
            pv = jnp.dot(pb[rows, :], head_values(hd), preferred_element_type=F32)
            acc_ref[rows, :] = alpha[rows, :] * acc_ref[rows, :] + pv

    s_pages = [jnp.dot(qbd, kpages[p_i][...].astype(BF16), preferred_element_type=F32) for p_i in range(P)]
    kpos = j * (P * PAGE_SIZE) + lax.broadcasted_iota(jnp.int32, (1, P * PAGE_SIZE), 1)

    def cache_values(hd):
        rows = [vpages[p_i][pl.ds(hd, PAGE_SIZE, stride=N_HEADS), :] for p_i in range(P)]
        return jnp.concatenate(rows, axis=0).astype(BF16)

    update(jnp.concatenate(s_pages, axis=1), kpos, False, cache_values)

    @pl.when(j == pl.num_programs(1) - 1)
    def _():
        knew_pad_ref[...] = jnp.zeros(knew_pad_ref.shape, F32)
        vnew_pad_ref[...] = jnp.zeros(vnew_pad_ref.shape, F32)
        knew_pad_ref[0:L, :] = knew_ref[...]
        vnew_pad_ref[0:L, :] = vnew_ref[...]
        s_new = lax.dot_general(qbd, knew_pad_ref[...].astype(BF16), (((1,), (1,)), ((), ())),
                                preferred_element_type=F32)
        kpos_new = past_len + lax.broadcasted_iota(jnp.int32, (1, PAGE_SIZE), 1)
        update(s_new, kpos_new, True, lambda hd: vnew_pad_ref[:, hd * V_DIM:(hd + 1) * V_DIM].astype(BF16))
        lam = _diff_lambda(lam_ref, layer)
        on = acc_ref[...] / l_ref[...]
        for hd in range(N_HEADS):
            r0 = hd * RH
            o = on[r0:r0 + L, :] - lam * on[r0 + L:r0 + 2 * L, :]
            o = _rms(o, sg_ref[...]) * (1.0 - _lambda_init(layer))
            o_ref[:, hd * V_DIM:(hd + 1) * V_DIM] = o.astype(o_ref.dtype)


def _sample_attn(page_table, qbd, rslope, rqpos, k_new, v_new, lam_params, subln_gain, cache_kt, cache_vr,
                 *, layer, n_seq, L):
    n_pages = page_table.shape[1]
    P = PAGES_PER_STEP
    n_steps = n_pages // P
    R = N_HEADS * 2 * L

    def page_spec(p_i):
        return pl.BlockSpec((None, None, PAGE_SIZE * N_HEADS, V_DIM),
                            lambda b, j, pt: (layer, pt[b, j * P + p_i], 0, 0))

    grid_spec = pltpu.PrefetchScalarGridSpec(
        num_scalar_prefetch=1,
        grid=(n_seq, n_steps),
        in_specs=[
            pl.BlockSpec((None, R, D_MODEL), lambda b, j, pt: (b, 0, 0)),
            pl.BlockSpec((R, 1), lambda b, j, pt: (0, 0)),
            pl.BlockSpec((R, 1), lambda b, j, pt: (0, 0)),
            pl.BlockSpec((L, D_MODEL), lambda b, j, pt: (b, 0)),
            pl.BlockSpec((L, D_MODEL), lambda b, j, pt: (b, 0)),
            pl.BlockSpec((4, HEAD_DIM), lambda b, j, pt: (0, 0)),
            pl.BlockSpec((1, V_DIM), lambda b, j, pt: (0, 0)),
        ] + [page_spec(p_i) for p_i in range(P)] + [page_spec(p_i) for p_i in range(P)],
        out_specs=pl.BlockSpec((L, D_MODEL), lambda b, j, pt: (b, 0)),
        scratch_shapes=[pltpu.VMEM((R, 1), F32), pltpu.VMEM((R, 1), F32), pltpu.VMEM((R, V_DIM), F32),
                        pltpu.VMEM((PAGE_SIZE, D_MODEL), F32), pltpu.VMEM((PAGE_SIZE, D_MODEL), F32)],
    )
    return pl.pallas_call(
        functools.partial(_sample_attn_kernel, layer=layer, n_pages_step=P, past_len=n_pages * PAGE_SIZE, L=L),
        grid_spec=grid_spec,
        out_shape=jax.ShapeDtypeStruct((n_seq * L, D_MODEL), F32),
        compiler_params=pltpu.CompilerParams(
            dimension_semantics=("arbitrary", "arbitrary"), vmem_limit_bytes=VMEM_LIMIT_BYTES),
        name="sample_attn",
    )(page_table, qbd, rslope, rqpos, k_new, v_new, lam_params, subln_gain,
      *([cache_kt] * P), *([cache_vr] * P))


def _mix_kernel(x_ref, a_ref, o_ref, gn_ref, wg_ref, bg_ref, wbp_ref, wba_ref, wout_ref, gpost_ref, y_ref):
    x = x_ref[...]
    h = _rms(x, gn_ref[...]).astype(BF16)
    g = jax.nn.sigmoid(jnp.dot(h, wg_ref[...], preferred_element_type=F32) + bg_ref[...])
    bp = jnp.dot(a_ref[...].astype(BF16), wbp_ref[...], preferred_element_type=F32)
    ba = jnp.dot(o_ref[...].astype(BF16), wba_ref[...], preferred_element_type=F32)
    mix = g[:, 0:D_MODEL] * bp + g[:, D_MODEL:2 * D_MODEL] * ba
    y = jnp.dot(mix.astype(BF16), wout_ref[...], preferred_element_type=F32)
    y_ref[...] = x + _rms(y, gpost_ref[...])


def _mix(x2d, a, o, gn, wg, bg, wbp, wba, wout, gpost, *, tile):
    N = x2d.shape[0]
    row = lambda i: (i, 0)
    return pl.pallas_call(
        _mix_kernel,
        grid=(N // tile,),
        in_specs=[
            pl.BlockSpec((tile, D_MODEL), row),
            pl.BlockSpec((tile, D_MODEL), row),
            pl.BlockSpec((tile, D_MODEL), row),
            _const_spec((1, D_MODEL)),
            _const_spec((D_MODEL, 2 * D_MODEL)),
            _const_spec((1, 2 * D_MODEL)),
            _const_spec((D_MODEL, D_MODEL)),
            _const_spec((D_MODEL, D_MODEL)),
            _const_spec((D_MODEL, D_MODEL)),
            _const_spec((1, D_MODEL)),
        ],
        out_specs=pl.BlockSpec((tile, D_MODEL), row),
        out_shape=jax.ShapeDtypeStruct((N, D_MODEL), F32),
        compiler_params=pltpu.CompilerParams(
            dimension_semantics=("arbitrary",), vmem_limit_bytes=VMEM_LIMIT_BYTES),
        name="mix",
    )(x2d, a, o, gn, wg, bg, wbp, wba, wout, gpost)


def _ffn_kernel(x_ref, prev_ref, gn_ref, wup_ref, cw_ref, cb_ref, wdown_ref, gpost_ref,
                y_ref, fstate_ref, ext_ref, *, G, T):
    t = pl.program_id(1)
    M = G * T
    x = x_ref[...]
    hn = _rms(x, gn_ref[...]).astype(BF16)

    @pl.when(t == 0)
    def _():
        ext_ref[:, 0:CONV_HALO, :] = prev_ref[...]

    gpre = jnp.dot(hn, wup_ref[:, 0:D_FF], preferred_element_type=F32)
    ext_ref[:, CONV_HALO:CONV_HALO + T, :] = gpre.reshape(G, T, D_FF)
    val = jnp.dot(hn, wup_ref[:, D_FF:2 * D_FF], preferred_element_type=F32)

    cw = cw_ref[...]
    c = cb_ref[...].reshape(1, 1, D_FF)
    for j in range(CONV_W):
        off = CONV_HALO - (CONV_W - 1) + j
        c = c + cw[j:j + 1, :].reshape(1, 1, D_FF) * ext_ref[:, off:off + T, :]
    act = jax.nn.gelu(c, approximate=True).reshape(M, D_FF) * val
    f = jnp.dot(act.astype(BF16), wdown_ref[...], preferred_element_type=F32)
    y_ref[...] = x + _rms(f, gpost_ref[...])

    tail = ext_ref[:, T:T + CONV_HALO, :]
    fstate_ref[...] = tail
    ext_ref[:, 0:CONV_HALO, :] = tail


def _ffn(x2d, prev, gn, wup, cw, cb, wdown, gpost, *, n_seq, seq_len, G, T):
    N = x2d.shape[0]
    M = G * T
    tiles_per_seq = seq_len // T
    n_outer = n_seq // G
    row = lambda b, t: (b * tiles_per_seq + t, 0)
    return pl.pallas_call(
        functools.partial(_ffn_kernel, G=G, T=T),
        grid=(n_outer, tiles_per_seq),
        in_specs=[
            pl.BlockSpec((M, D_MODEL), row),
            pl.BlockSpec((G, CONV_HALO, D_FF), lambda b, t: (b, 0, 0)),
            _const_spec((1, D_MODEL)),
            _const_spec((D_MODEL, 2 * D_FF)),
            _const_spec((CONV_W, D_FF)),
            _const_spec((1, D_FF)),
            _const_spec((D_FF, D_MODEL)),
            _const_spec((1, D_MODEL)),
        ],
        out_specs=(
            pl.BlockSpec((M, D_MODEL), row),
            pl.BlockSpec((G, CONV_HALO, D_FF), lambda b, t: (b, 0, 0)),
        ),
        out_shape=(
            jax.ShapeDtypeStruct((N, D_MODEL), F32),
            jax.ShapeDtypeStruct((n_seq, CONV_HALO, D_FF), F32),
        ),
        scratch_shapes=[pltpu.VMEM((G, CONV_HALO + T, D_FF), F32)],
        compiler_params=pltpu.CompilerParams(
            dimension_semantics=("arbitrary", "arbitrary"), vmem_limit_bytes=VMEM_LIMIT_BYTES),
        name="ffn",
    )(x2d, prev, gn, wup, cw, cb, wdown, gpost)


def _block_diag_queries(qm, n_seq, L):
    q = qm.reshape(2, n_seq, L, N_HEADS, V_DIM)
    eye = jnp.eye(N_HEADS, dtype=qm.dtype)
    qbd = jnp.einsum('cbihd,hg->bgcihd', q, eye)
    return qbd.reshape(n_seq, N_HEADS * 2 * L, D_MODEL)


def kernel(x_prompt, x_sample, cache_k, cache_v, state_pool, state_ffn, page_table, norm_pre_mix, w_in,
           w_pool_group, pool_scale, lambda_q1, lambda_k1, lambda_q2, lambda_k2, subln_gain, w_gate, b_gate,
           w_branch_pool, w_branch_attn, w_out, norm_post_mix, norm_pre_ffn, w_up, conv_w, conv_b, w_down,
           norm_post_ffn):
    depth = w_in.shape[0]
    n_p, s_p, _ = x_prompt.shape
    n_s, s_s, _ = x_sample.shape
    n_pages = page_table.shape[1]
    past_len = n_pages * PAGE_SIZE
    n_pool = cache_k.shape[1]

    cache_kt = jnp.transpose(cache_k, (0, 1, 3, 4, 5, 2)).reshape(depth, n_pool, D_MODEL, PAGE_SIZE)
    cache_vr = cache_v.reshape(depth, n_pool, PAGE_SIZE * N_HEADS, V_DIM)
    slopes = jnp.exp2(-8.0 * jnp.arange(1, N_HEADS + 1, dtype=F32) / N_HEADS)
    rslope = jnp.repeat(slopes, 2 * s_s).reshape(N_HEADS * 2 * s_s, 1)
    rqpos = jnp.tile(past_len + jnp.arange(s_s, dtype=jnp.int32), N_HEADS * 2).reshape(N_HEADS * 2 * s_s, 1)

    pool_prev_p = jnp.zeros((n_p, POOL_HALO, D_MODEL), F32)
    ffn_prev_p = jnp.zeros((n_p, CONV_HALO, D_FF), F32)
    pool_prev_s = jnp.pad(state_pool, ((0, 0), (0, 0), (POOL_HALO - POOL_PAD, 0), (0, 0)))
    ffn_prev_s = jnp.pad(state_ffn, ((0, 0), (0, 0), (CONV_HALO - (CONV_W - 1), 0), (0, 0)))

    xp = x_prompt.reshape(n_p * s_p, D_MODEL)
    xs = x_sample.reshape(n_s * s_s, D_MODEL)
    outs = {name: [] for name in ("pp", "fp", "ks", "vs", "ps", "fs")}
    kt_all = v_all = None

    for l in range(depth):
        gn = norm_pre_mix[l].reshape(1, D_MODEL)
        win = w_in[l].astype(BF16)
        wkt = w_in[l, :, 2 * D_MODEL:3 * D_MODEL].T.astype(BF16)
        wpool = w_pool_group[l].astype(BF16)
        pscale = pool_scale[l].reshape(1, D_MODEL)
        lam_params = jnp.stack([lambda_q1[l], lambda_k1[l], lambda_q2[l], lambda_k2[l]])
        sg = subln_gain[l].reshape(1, V_DIM)
        mix_w = (gn, w_gate[l].astype(BF16), b_gate[l].reshape(1, 2 * D_MODEL), w_branch_pool[l].astype(BF16),
                 w_branch_attn[l].astype(BF16), w_out[l].astype(BF16), norm_post_mix[l].reshape(1, D_MODEL))
        ffn_w = (norm_pre_ffn[l].reshape(1, D_MODEL), w_up[l].astype(BF16), conv_w[l], conv_b[l].reshape(1, D_FF),
                 w_down[l].astype(BF16), norm_post_ffn[l].reshape(1, D_MODEL))

        qm, kt_all, ktb, v_all, vb, a, pstate = _in_proj_prompt(
            xp, pool_prev_p, gn, win, wkt, wpool, pscale, kt_all, v_all,
            layer=l, depth=depth, n_seq=n_p, seq_len=s_p)
        o = _prompt_attn(slopes, qm, ktb, vb, lam_params, sg, layer=l, n_seq=n_p, seq_len=s_p)
        x1 = _mix(xp, a, o, *mix_w, tile=ROW_TILE_MIX)
        xp, fstate = _ffn(x1, ffn_prev_p, *ffn_w, n_seq=n_p, seq_len=s_p, G=1, T=ROW_TILE_FFN)
        outs["pp"].append(pstate[:, POOL_HALO - POOL_PAD:, :])
        outs["fp"].append(fstate[:, CONV_HALO - (CONV_W - 1):, :])

        qm, k, v, a, pstate = _in_proj_sample(xs, pool_prev_s[l], gn, win, wpool, pscale,
                                              n_seq=n_s, seq_len=s_s, pos0=past_len)
        qbd = _block_diag_queries(qm, n_s, s_s)
        o = _sample_attn(page_table, qbd, rslope, rqpos, k, v, lam_params, sg, cache_kt, cache_vr,
                         layer=l, n_seq=n_s, L=s_s)
        x1 = _mix(xs, a, o, *mix_w, tile=n_s * s_s)
        xs, fstate = _ffn(x1, ffn_prev_s[l], *ffn_w, n_seq=n_s, seq_len=s_s, G=n_s, T=s_s)
        outs["ks"].append(k.reshape(n_s, s_s, N_HEADS, 2, HEAD_DIM))
        outs["vs"].append(v.reshape(n_s, s_s, N_HEADS, V_DIM))
        outs["ps"].append(pstate[:, POOL_HALO - POOL_PAD:, :])
        outs["fs"].append(fstate[:, CONV_HALO - (CONV_W - 1):, :])

    st = {name: jnp.stack(v) for name, v in outs.items()}
    k_prompt = jnp.transpose(kt_all.reshape(depth, n_p, N_HEADS, 2, HEAD_DIM, s_p), (0, 1, 5, 2, 3, 4))
    v_prompt = v_all.reshape(depth, n_p, s_p, N_HEADS, V_DIM)
    return (xp.reshape(n_p, s_p, D_MODEL), xs.reshape(n_s, s_s, D_MODEL),
            k_prompt, v_prompt, st["pp"], st["fp"], st["ks"], st["vs"], st["ps"], st["fs"])
```

```python
import functools
import math

import jax
import jax.numpy as jnp
from jax import lax
from jax.experimental import pallas as pl
from jax.experimental.pallas import tpu as pltpu

F32 = jnp.float32
BF16 = jnp.bfloat16

D_MODEL = 1024
N_HEADS = 8
HEAD_DIM = 64
V_DIM = 2 * HEAD_DIM
ATTN_SCALE = HEAD_DIM ** -0.5
LOG2E = math.log2(math.e)
POOL_WINDOWS = (2, 4, 8, 16)
POOL_GROUP = D_MODEL // len(POOL_WINDOWS)
POOL_PAD = max(POOL_WINDOWS) - 1
POOL_HALO = 16
D_FF = 2816
CONV_W = 3
CONV_HALO = 8
NORM_EPS = 1e-6
PAGE_SIZE = 128
NEG_BIG = -1e30

VMEM_LIMIT_BYTES = 56 * 1024 * 1024

ROW_TILE_PROJ = 512
ROW_TILE_MIX = 512
ROW_TILE_FFN = 256
ATTN_TILE = 256
PAGES_PER_STEP = 8
PAGES_PER_GROUP = 8


def _lambda_init(layer):
    return 0.8 - 0.6 * math.exp(-0.3 * layer)


def _rms(x, g):
    ms = jnp.mean(x * x, axis=-1, keepdims=True)
    return x * lax.rsqrt(ms + NORM_EPS) * g


def _const_spec(shape):
    nd = len(shape)
    return pl.BlockSpec(shape, lambda *_: (0,) * nd, pipeline_mode=pl.Buffered(1))


def _diff_lambda(lam_ref, layer):
    lp = lam_ref[...]
    a = jnp.sum(lp[0:1, :] * lp[1:2, :], axis=-1, keepdims=True)
    b = jnp.sum(lp[2:3, :] * lp[3:4, :], axis=-1, keepdims=True)
    return jnp.exp(a) - jnp.exp(b) + _lambda_init(layer)


def _store_queries(h, win_ref, qm_ref):
    zq = jnp.dot(h, win_ref[:, D_MODEL:2 * D_MODEL], preferred_element_type=F32) * (ATTN_SCALE * LOG2E)
    lane = lax.broadcasted_iota(jnp.int32, (1, D_MODEL), 1)
    first_map = (lane % V_DIM) < HEAD_DIM
    qm_ref[0] = jnp.where(first_map, zq, 0.0).astype(BF16)
    qm_ref[1] = jnp.where(first_map, 0.0, zq).astype(BF16)


def _pool_branch(t, h, prev_ref, win_ref, wpool_ref, pscale_ref, a_ref, pstate_ref, ext_ref, *, G, T, pos0):
    M = G * T

    @pl.when(t == 0)
    def _():
        ext_ref[:, 0:POOL_HALO, :] = prev_ref[...]

    zu = jnp.dot(h, win_ref[:, 0:D_MODEL], preferred_element_type=F32)
    ext_ref[:, POOL_HALO:POOL_HALO + T, :] = zu.reshape(G, T, D_MODEL)

    pos = pos0 + t * T + lax.broadcasted_iota(jnp.int32, (1, T, 1), 1)
    for g, w in enumerate(POOL_WINDOWS):
        cols = slice(g * POOL_GROUP, (g + 1) * POOL_GROUP)
        u_new = ext_ref[:, POOL_HALO:POOL_HALO + T, cols]
        win = u_new
        for j in range(1, w):
            win = win + ext_ref[:, POOL_HALO - j:POOL_HALO - j + T, cols]
        cnt = jnp.minimum(pos + 1, w).astype(F32)
        d = (win / cnt - u_new).astype(BF16).reshape(M, POOL_GROUP)
        y = jnp.dot(d, wpool_ref[g], preferred_element_type=F32)
        a_ref[:, cols] = (y * pscale_ref[:, cols]).astype(BF16)

    tail = ext_ref[:, T:T + POOL_HALO, :]
    pstate_ref[...] = tail
    ext_ref[:, 0:POOL_HALO, :] = tail


def _in_proj_prompt_kernel(*refs, T, aliased):
    n_in = 7 + (2 if aliased else 0)
    x_ref, prev_ref, gn_ref, win_ref, wkt_ref, wpool_ref, pscale_ref = refs[:7]
    qm_ref, kt_ref, ktb_ref, v_ref, vb_ref, a_ref, pstate_ref, ext_ref = refs[n_in:]
    t = pl.program_id(1)
    h = _rms(x_ref[...], gn_ref[...]).astype(BF16)
    _store_queries(h, win_ref, qm_ref)
    zkt = lax.dot_general(wkt_ref[...], h, (((1,), (1,)), ((), ())), preferred_element_type=F32)
    kt_ref[...] = zkt
    for c in range(T // ATTN_TILE):
        ktb_ref[c] = zkt[:, c * ATTN_TILE:(c + 1) * ATTN_TILE].astype(BF16)
    zv = jnp.dot(h, win_ref[:, 3 * D_MODEL:4 * D_MODEL], preferred_element_type=F32)
    v_ref[...] = zv
    vb_ref[...] = zv.astype(BF16)
    _pool_branch(t, h, prev_ref, win_ref, wpool_ref, pscale_ref, a_ref, pstate_ref, ext_ref, G=1, T=T, pos0=0)


def _in_proj_prompt(x2d, prev, gn, win, wkt, wpool, pscale, kt_all, v_all, *, layer, depth, n_seq, seq_len):
    N = x2d.shape[0]
    T = ROW_TILE_PROJ
    tiles_per_seq = seq_len // T
    blocks_per_tile = T // ATTN_TILE
    aliased = kt_all is not None
    row = lambda b, t: (b * tiles_per_seq + t, 0)
    out_shape = (
        jax.ShapeDtypeStruct((2, N, D_MODEL), BF16),
        jax.ShapeDtypeStruct((depth, n_seq, D_MODEL, seq_len), F32),
        jax.ShapeDtypeStruct((n_seq, seq_len // ATTN_TILE, D_MODEL, ATTN_TILE), BF16),
        jax.ShapeDtypeStruct((depth, N, D_MODEL), F32),
        jax.ShapeDtypeStruct((N, D_MODEL), BF16),
        jax.ShapeDtypeStruct((N, D_MODEL), BF16),
        jax.ShapeDtypeStruct((n_seq, POOL_HALO, D_MODEL), F32),
    )
    in_specs = [
        pl.BlockSpec((T, D_MODEL), row),
        pl.BlockSpec((1, POOL_HALO, D_MODEL), lambda b, t: (b, 0, 0)),
        _const_spec((1, D_MODEL)),
        _const_spec((D_MODEL, 4 * D_MODEL)),
        _const_spec((D_MODEL, D_MODEL)),
        _const_spec((len(POOL_WINDOWS), POOL_GROUP, POOL_GROUP)),
        _const_spec((1, D_MODEL)),
    ]
    args = [x2d, prev, gn, win, wkt, wpool, pscale]
    aliases = {}
    if aliased:
        in_specs += [pl.BlockSpec(memory_space=pl.ANY), pl.BlockSpec(memory_space=pl.ANY)]
        args += [kt_all, v_all]
        aliases = {7: 1, 8: 3}
    return pl.pallas_call(
        functools.partial(_in_proj_prompt_kernel, T=T, aliased=aliased),
        grid=(n_seq, tiles_per_seq),
        in_specs=in_specs,
        out_specs=(
            pl.BlockSpec((2, T, D_MODEL), lambda b, t: (0, b * tiles_per_seq + t, 0)),
            pl.BlockSpec((None, None, D_MODEL, T), lambda b, t: (layer, b, 0, t)),
            pl.BlockSpec((None, blocks_per_tile, D_MODEL, ATTN_TILE), lambda b, t: (b, t, 0, 0)),
            pl.BlockSpec((None, T, D_MODEL), lambda b, t: (layer, b * tiles_per_seq + t, 0)),
            pl.BlockSpec((T, D_MODEL), row),
            pl.BlockSpec((T, D_MODEL), row),
            pl.BlockSpec((1, POOL_HALO, D_MODEL), lambda b, t: (b, 0, 0)),
        ),
        out_shape=out_shape,
        scratch_shapes=[pltpu.VMEM((1, POOL_HALO + T, D_MODEL), F32)],
        input_output_aliases=aliases,
        compiler_params=pltpu.CompilerParams(
            dimension_semantics=("arbitrary", "arbitrary"), vmem_limit_bytes=VMEM_LIMIT_BYTES),
        name="in_proj_prompt",
    )(*args)


def _in_proj_sample_kernel(x_ref, prev_ref, gn_ref, win_ref, wpool_ref, pscale_ref,
                           qm_ref, k_ref, v_ref, a_ref, pstate_ref, ext_ref, *, G, T, pos0):
    h = _rms(x_ref[...], gn_ref[...]).astype(BF16)
    _store_queries(h, win_ref, qm_ref)
    k_ref[...] = jnp.dot(h, win_ref[:, 2 * D_MODEL:3 * D_MODEL], preferred_element_type=F32)
    v_ref[...] = jnp.dot(h, win_ref[:, 3 * D_MODEL:4 * D_MODEL], preferred_element_type=F32)
    _pool_branch(pl.program_id(0), h, prev_ref, win_ref, wpool_ref, pscale_ref, a_ref, pstate_ref, ext_ref,
                 G=G, T=T, pos0=pos0)


def _in_proj_sample(x2d, prev, gn, win, wpool, pscale, *, n_seq, seq_len, pos0):
    N = x2d.shape[0]
    whole2 = lambda i: (0, 0)
    whole3 = lambda i: (0, 0, 0)
    return pl.pallas_call(
        functools.partial(_in_proj_sample_kernel, G=n_seq, T=seq_len, pos0=pos0),
        grid=(1,),
        in_specs=[
            pl.BlockSpec((N, D_MODEL), whole2),
            pl.BlockSpec((n_seq, POOL_HALO, D_MODEL), whole3),
            _const_spec((1, D_MODEL)),
            _const_spec((D_MODEL, 4 * D_MODEL)),
            _const_spec((len(POOL_WINDOWS), POOL_GROUP, POOL_GROUP)),
            _const_spec((1, D_MODEL)),
        ],
        out_specs=(
            pl.BlockSpec((2, N, D_MODEL), whole3),
            pl.BlockSpec((N, D_MODEL), whole2),
            pl.BlockSpec((N, D_MODEL), whole2),
            pl.BlockSpec((N, D_MODEL), whole2),
            pl.BlockSpec((n_seq, POOL_HALO, D_MODEL), whole3),
        ),
        out_shape=(
            jax.ShapeDtypeStruct((2, N, D_MODEL), BF16),
            jax.ShapeDtypeStruct((N, D_MODEL), F32),
            jax.ShapeDtypeStruct((N, D_MODEL), F32),
            jax.ShapeDtypeStruct((N, D_MODEL), BF16),
            jax.ShapeDtypeStruct((n_seq, POOL_HALO, D_MODEL), F32),
        ),
        scratch_shapes=[pltpu.VMEM((n_seq, POOL_HALO + seq_len, D_MODEL), F32)],
        compiler_params=pltpu.CompilerParams(
            dimension_semantics=("arbitrary",), vmem_limit_bytes=VMEM_LIMIT_BYTES),
        name="in_proj_sample",
    )(x2d, prev, gn, win, wpool, pscale)


def _prompt_attn_kernel(slope_ref, qm_ref, kt_ref, vb_ref, lam_ref, sg_ref, o_ref, *scratch, layer, TQ, nq):
    s_ref, macc_ref, mb_ref, lacc_ref, r_ref, acc_ref = (scratch[3 * i:3 * i + 3] for i in range(6))
    hd = pl.program_id(1)
    HALF = TQ // 2
    slope2 = slope_ref[hd] * LOG2E
    key_iota = lax.broadcasted_iota(jnp.int32, (1, TQ), 1)

    def scores(T, slot, j, masked):
        q2 = qm_ref[:, T * TQ:(T + 1) * TQ, :].reshape(2 * TQ, V_DIM)
        s = jnp.dot(q2, kt_ref[j], preferred_element_type=F32)
        s = s + slope2 * ((j - T) * TQ + key_iota).astype(F32)
        if masked:
            qrow = lax.broadcasted_iota(jnp.int32, (2 * TQ, 1), 0) % TQ
            s = jnp.where(key_iota <= qrow, s, NEG_BIG)
        s_ref[slot][j] = s
        macc_ref[slot][...] = jnp.maximum(macc_ref[slot][...], jnp.maximum(s[:, 0:HALF], s[:, HALF:TQ]))

    def probs(slot, j):
        mb = mb_ref[slot][...]
        p_lo = jnp.exp2(s_ref[slot][j, :, 0:HALF] - mb)
        p_hi = jnp.exp2(s_ref[slot][j, :, HALF:TQ] - mb)
        s_ref[slot][j, :, 0:HALF] = p_lo
        s_ref[slot][j, :, HALF:TQ] = p_hi
        lacc_ref[slot][...] += p_lo + p_hi

    def weighted_values(slot, j):
        r0 = r_ref[slot][0:TQ, :]
        r1 = r_ref[slot][TQ:2 * TQ, :]
        a_lo = s_ref[slot][j, 0:TQ, 0:HALF] * r0 - s_ref[slot][j, TQ:2 * TQ, 0:HALF] * r1
        a_hi = s_ref[slot][j, 0:TQ, HALF:TQ] * r0 - s_ref[slot][j, TQ:2 * TQ, HALF:TQ] * r1
        a = jnp.concatenate([a_lo, a_hi], axis=-1).astype(BF16)
        vj = vb_ref[pl.ds(j * TQ if isinstance(j, int) else pl.multiple_of(j * TQ, TQ), TQ), :]
        acc_ref[slot][...] += jnp.dot(a, vj, preferred_element_type=F32)

    def stage(T, do_s, do_p, do_v):
        slot_s, slot_p, slot_v = T % 3, (T + 2) % 3, (T + 1) % 3
        if do_s:
            macc_ref[slot_s][...] = jnp.full(macc_ref[slot_s].shape, NEG_BIG, F32)
        if do_p:
            lacc_ref[slot_p][...] = jnp.zeros(lacc_ref[slot_p].shape, F32)
        if do_v:
            acc_ref[slot_v][...] = jnp.zeros(acc_ref[slot_v].shape, F32)

        def fused(j, c):
            if do_s:
                scores(T, slot_s, j, False)
            if do_p:
                probs(slot_p, j)
            if do_v:
                weighted_values(slot_v, j)
            return c

        if do_v:
            lax.fori_loop(0, T - 1, fused, 0, unroll=2)
        if do_p:
            if do_s:
                scores(T, slot_s, T - 1, False)
            probs(slot_p, T - 1)
        if do_s:
            scores(T, slot_s, T, True)
            m = jnp.max(macc_ref[slot_s][...], axis=-1, keepdims=True)
            mb_ref[slot_s][...] = jnp.broadcast_to(m, mb_ref[slot_s].shape)
        if do_p:
            ones = jnp.ones((HALF, HALF), BF16)
            lsum = jnp.dot(lacc_ref[slot_p][...].astype(BF16), ones, preferred_element_type=F32)
            r_ref[slot_p][0:TQ, :] = 1.0 / lsum[0:TQ]
            r_ref[slot_p][TQ:2 * TQ, :] = _diff_lambda(lam_ref, layer) / lsum[TQ:2 * TQ]
        if do_v:
            o = _rms(acc_ref[slot_v][...], sg_ref[...]) * (1.0 - _lambda_init(layer))
            o_ref[(T - 2) * TQ:(T - 1) * TQ, :] = o.astype(o_ref.dtype)

    for T in range(nq + 2):
        stage(T, T < nq, 1 <= T <= nq, T >= 2)


def _prompt_attn(slopes, qm, ktb, vb, lam_params, subln_gain, *, layer, n_seq, seq_len):
    N = vb.shape[0]
    TQ = ATTN_TILE
    nq = seq_len // TQ
    assert nq >= 2
    return pl.pallas_call(
        functools.partial(_prompt_attn_kernel, layer=layer, TQ=TQ, nq=nq),
        grid=(n_seq, N_HEADS),
        in_specs=[
            pl.BlockSpec(memory_space=pltpu.SMEM),
            pl.BlockSpec((2, seq_len, V_DIM), lambda b, h: (0, b, h)),
            pl.BlockSpec((None, nq, V_DIM, TQ), lambda b, h: (b, 0, h, 0)),
            pl.BlockSpec((seq_len, V_DIM), lambda b, h: (b, h)),
            _const_spec((4, HEAD_DIM)),
            _const_spec((1, V_DIM)),
        ],
        out_specs=pl.BlockSpec((seq_len, V_DIM), lambda b, h: (b, h)),
        out_shape=jax.ShapeDtypeStruct((N, D_MODEL), BF16),
        scratch_shapes=(
            [pltpu.VMEM((nq, 2 * TQ, TQ), F32)] * 3
            + [pltpu.VMEM((2 * TQ, TQ // 2), F32)] * 12
            + [pltpu.VMEM((TQ, V_DIM), F32)] * 3
        ),
        compiler_params=pltpu.CompilerParams(
            dimension_semantics=("arbitrary", "arbitrary"), vmem_limit_bytes=VMEM_LIMIT_BYTES),
        name="prompt_attn",
    )(slopes, qm, ktb, vb, lam_params, subln_gain)


def _sample_attn_kernel(pt_ref, qbd_ref, rslope_ref, rqpos_ref, knew_ref, vnew_ref, lam_ref, sg_ref, *rest,
                        layer, n_pages_step, past_len, L):
    P = n_pages_step
    kpages = rest[:P]
    vpages = rest[P:2 * P]
    o_ref = rest[2 * P]
    m_ref, l_ref, acc_ref, knew_pad_ref, vnew_pad_ref = rest[2 * P + 1:]
    j = pl.program_id(1)
    RH = 2 * L
    qbd = qbd_ref[...]
    rslope2 = rslope_ref[...] * LOG2E
    rqpos = rqpos_ref[...]

    @pl.when(j == 0)
    def _():
        m_ref[...] = jnp.full(m_ref.shape, NEG_BIG, F32)
        l_ref[...] = jnp.zeros(l_ref.shape, F32)
        acc_ref[...] = jnp.zeros(acc_ref.shape, F32)

    def update(state, s, kpos, causal, head_values):
        m_old, l_old, acc_old = state
        dist = (rqpos - kpos).astype(F32)
        s = s - rslope2 * dist
        if causal:
            s = jnp.where(dist >= 0.0, s, NEG_BIG)
        m_new = jnp.maximum(m_old, jnp.max(s, axis=-1, keepdims=True))
        alpha = jnp.exp2(m_old - m_new)
        p = jnp.exp2(s - m_new)
        l_new = alpha * l_old + jnp.sum(p, axis=-1, keepdims=True)
        pb = p.astype(BF16)
        acc_new = []
        for hp in range(N_HEADS // 2):
            rows = slice(2 * hp * RH, (2 * hp + 2) * RH)
            values = jnp.concatenate([head_values(2 * hp), head_values(2 * hp + 1)], axis=1)
            pv = jnp.dot(pb[rows, :], values, preferred_element_type=F32)
            pv = jnp.concatenate([pv[0:RH, 0:V_DIM], pv[RH:2 * RH, V_DIM:2 * V_DIM]], axis=0)
            acc_new.append(alpha[rows, :] * acc_old[rows, :] + pv)
        return m_new, l_new, jnp.concatenate(acc_new, axis=0)

    state = (m_ref[...], l_ref[...], acc_ref[...])
    GP = PAGES_PER_GROUP
    for g in range(P // GP):
        group = range(g * GP, (g + 1) * GP)
        kt = jnp.concatenate([kpages[p_i][...].astype(BF16) for p_i in group], axis=1)
        s = jnp.dot(qbd, kt, preferred_element_type=F32)
        kpos = (j * P + g * GP) * PAGE_SIZE + lax.broadcasted_iota(jnp.int32, (1, GP * PAGE_SIZE), 1)

        def cache_values(hd, group=group):
            rows = [vpages[p_i][pl.ds(hd, PAGE_SIZE, stride=N_HEADS), :] for p_i in group]
            return jnp.concatenate(rows, axis=0).astype(BF16)

        state = update(state, s, kpos, False, cache_values)
    m_ref[...], l_ref[...], acc_ref[...] = state

    @pl.when(j == pl.num_programs(1) - 1)
    def _():
        knew_pad_ref[...] = jnp.zeros(knew_pad_ref.shape, F32)
        vnew_pad_ref[...] = jnp.zeros(vnew_pad_ref.shape, F32)
        knew_pad_ref[0:L, :] = knew_ref[...]
        vnew_pad_ref[0:L, :] = vnew_ref[...]
        s_new = lax.dot_general(qbd, knew_pad_ref[...].astype(BF16), (((1,), (1,)), ((), ())),
                                preferred_element_type=F32)
        kpos_new = past_len + lax.broadcasted_iota(jnp.int32, (1, PAGE_SIZE), 1)
        _, l_fin, acc_fin = update((m_ref[...], l_ref[...], acc_ref[...]), s_new, kpos_new, True,
                                   lambda hd: vnew_pad_ref[:, hd * V_DIM:(hd + 1) * V_DIM].astype(BF16))
        lam = _diff_lambda(lam_ref, layer)
        on = acc_fin / l_fin
        for hd in range(N_HEADS):
            r0 = hd * RH
            o = on[r0:r0 + L, :] - lam * on[r0 + L:r0 + 2 * L, :]
            o = _rms(o, sg_ref[...]) * (1.0 - _lambda_init(layer))
            o_ref[:, hd * V_DIM:(hd + 1) * V_DIM] = o.astype(o_ref.dtype)


def _sample_attn(page_table, qbd, rslope, rqpos, k_new, v_new, lam_params, subln_gain, cache_kt, cache_vr,
                 *, layer, n_seq, L):
    n_pages = page_table.shape[1]
    P = PAGES_PER_STEP
    n_steps = n_pages // P
    R = N_HEADS * 2 * L

    def page_spec(p_i):
        return pl.BlockSpec((None, None, PAGE_SIZE * N_HEADS, V_DIM),
                            lambda b, j, pt: (layer, pt[b, j * P + p_i], 0, 0))

    grid_spec = pltpu.PrefetchScalarGridSpec(
        num_scalar_prefetch=1,
        grid=(n_seq, n_steps),
        in_specs=[
            pl.BlockSpec((None, R, D_MODEL), lambda b, j, pt: (b, 0, 0)),
            pl.BlockSpec((R, 1), lambda b, j, pt: (0, 0)),
            pl.BlockSpec((R, 1), lambda b, j, pt: (0, 0)),
            pl.BlockSpec((L, D_MODEL), lambda b, j, pt: (b, 0)),
            pl.BlockSpec((L, D_MODEL), lambda b, j, pt: (b, 0)),
            pl.BlockSpec((4, HEAD_DIM), lambda b, j, pt: (0, 0)),
            pl.BlockSpec((1, V_DIM), lambda b, j, pt: (0, 0)),
        ] + [page_spec(p_i) for p_i in range(P)] + [page_spec(p_i) for p_i in range(P)],
        out_specs=pl.BlockSpec((L, D_MODEL), lambda b, j, pt: (b, 0)),
        scratch_shapes=[pltpu.VMEM((R, 1), F32), pltpu.VMEM((R, 1), F32), pltpu.VMEM((R, V_DIM), F32),
                        pltpu.VMEM((PAGE_SIZE, D_MODEL), F32), pltpu.VMEM((PAGE_SIZE, D_MODEL), F32)],
    )
    return pl.pallas_call(
        functools.partial(_sample_attn_kernel, layer=layer, n_pages_step=P, past_len=n_pages * PAGE_SIZE, L=L),
        grid_spec=grid_spec,
        out_shape=jax.ShapeDtypeStruct((n_seq * L, D_MODEL), F32),
        compiler_params=pltpu.CompilerParams(
            dimension_semantics=("arbitrary", "arbitrary"), vmem_limit_bytes=VMEM_LIMIT_BYTES),
        name="sample_attn",
    )(page_table, qbd, rslope, rqpos, k_new, v_new, lam_params, subln_gain,
      *([cache_kt] * P), *([cache_vr] * P))


def _mix_kernel(x_ref, a_ref, o_ref, gn_ref, wg_ref, bg_ref, wbp_ref, wba_ref, wout_ref, gpost_ref, y_ref):
    x = x_ref[...]
    h = _rms(x, gn_ref[...]).astype(BF16)
    g = jax.nn.sigmoid(jnp.dot(h, wg_ref[...], preferred_element_type=F32) + bg_ref[...])
    bp = jnp.dot(a_ref[...].astype(BF16), wbp_ref[...], preferred_element_type=F32)
    ba = jnp.dot(o_ref[...].astype(BF16), wba_ref[...], preferred_element_type=F32)
    mix = g[:, 0:D_MODEL] * bp + g[:, D_MODEL:2 * D_MODEL] * ba
    y = jnp.dot(mix.astype(BF16), wout_ref[...], preferred_element_type=F32)
    y_ref[...] = x + _rms(y, gpost_ref[...])


def _mix(x2d, a, o, gn, wg, bg, wbp, wba, wout, gpost, *, tile):
    N = x2d.shape[0]
    row = lambda i: (i, 0)
    return pl.pallas_call(
        _mix_kernel,
        grid=(N // tile,),
        in_specs=[
            pl.BlockSpec((tile, D_MODEL), row),
            pl.BlockSpec((tile, D_MODEL), row),
            pl.BlockSpec((tile, D_MODEL), row),
            _const_spec((1, D_MODEL)),
            _const_spec((D_MODEL, 2 * D_MODEL)),
            _const_spec((1, 2 * D_MODEL)),
            _const_spec((D_MODEL, D_MODEL)),
            _const_spec((D_MODEL, D_MODEL)),
            _const_spec((D_MODEL, D_MODEL)),
            _const_spec((1, D_MODEL)),
        ],
        out_specs=pl.BlockSpec((tile, D_MODEL), row),
        out_shape=jax.ShapeDtypeStruct((N, D_MODEL), F32),
        compiler_params=pltpu.CompilerParams(
            dimension_semantics=("arbitrary",), vmem_limit_bytes=VMEM_LIMIT_BYTES),
        name="mix",
    )(x2d, a, o, gn, wg, bg, wbp, wba, wout, gpost)


def _ffn_kernel(x_ref, prev_ref, gn_ref, wup_ref, cw_ref, cb_ref, wdown_ref, gpost_ref,
                y_ref, fstate_ref, ext_ref, *, G, T):
    t = pl.program_id(1)
    M = G * T
    x = x_ref[...]
    hn = _rms(x, gn_ref[...]).astype(BF16)

    @pl.when(t == 0)
    def _():
        ext_ref[:, 0:CONV_HALO, :] = prev_ref[...]

    gpre = jnp.dot(hn, wup_ref[:, 0:D_FF], preferred_element_type=F32)
    ext_ref[:, CONV_HALO:CONV_HALO + T, :] = gpre.reshape(G, T, D_FF)
    val = jnp.dot(hn, wup_ref[:, D_FF:2 * D_FF], preferred_element_type=F32)

    cw = cw_ref[...]
    c = cb_ref[...].reshape(1, 1, D_FF)
    for j in range(CONV_W):
        off = CONV_HALO - (CONV_W - 1) + j
        c = c + cw[j:j + 1, :].reshape(1, 1, D_FF) * ext_ref[:, off:off + T, :]
    act = jax.nn.gelu(c, approximate=True).reshape(M, D_FF) * val
    f = jnp.dot(act.astype(BF16), wdown_ref[...], preferred_element_type=F32)
    y_ref[...] = x + _rms(f, gpost_ref[...])

    tail = ext_ref[:, T:T + CONV_HALO, :]
    fstate_ref[...] = tail
    ext_ref[:, 0:CONV_HALO, :] = tail


def _ffn(x2d, prev, gn, wup, cw, cb, wdown, gpost, *, n_seq, seq_len, G, T):
    N = x2d.shape[0]
    M = G * T
    tiles_per_seq = seq_len // T
    n_outer = n_seq // G
    row = lambda b, t: (b * tiles_per_seq + t, 0)
    return pl.pallas_call(
        functools.partial(_ffn_kernel, G=G, T=T),
        grid=(n_outer, tiles_per_seq),
        in_specs=[
            pl.BlockSpec((M, D_MODEL), row),
            pl.BlockSpec((G, CONV_HALO, D_FF), lambda b, t: (b, 0, 0)),
            _const_spec((1, D_MODEL)),
            _const_spec((D_MODEL, 2 * D_FF)),
            _const_spec((CONV_W, D_FF)),
            _const_spec((1, D_FF)),
            _const_spec((D_FF, D_MODEL)),
            _const_spec((1, D_MODEL)),
        ],
        out_specs=(
            pl.BlockSpec((M, D_MODEL), row),
            pl.BlockSpec((G, CONV_HALO, D_FF), lambda b, t: (b, 0, 0)),
        ),
        out_shape=(
            jax.ShapeDtypeStruct((N, D_MODEL), F32),
            jax.ShapeDtypeStruct((n_seq, CONV_HALO, D_FF), F32),
        ),
        scratch_shapes=[pltpu.VMEM((G, CONV_HALO + T, D_FF), F32)],
        compiler_params=pltpu.CompilerParams(
            dimension_semantics=("arbitrary", "arbitrary"), vmem_limit_bytes=VMEM_LIMIT_BYTES),
        name="ffn",
    )(x2d, prev, gn, wup, cw, cb, wdown, gpost)


def _block_diag_queries(qm, n_seq, L):
    q = qm.reshape(2, n_seq, L, N_HEADS, V_DIM)
    eye = jnp.eye(N_HEADS, dtype=qm.dtype)
    qbd = jnp.einsum('cbihd,hg->bgcihd', q, eye)
    return qbd.reshape(n_seq, N_HEADS * 2 * L, D_MODEL)


def kernel(x_prompt, x_sample, cache_k, cache_v, state_pool, state_ffn, page_table, norm_pre_mix, w_in,
           w_pool_group, pool_scale, lambda_q1, lambda_k1, lambda_q2, lambda_k2, subln_gain, w_gate, b_gate,
           w_branch_pool, w_branch_attn, w_out, norm_post_mix, norm_pre_ffn, w_up, conv_w, conv_b, w_down,
           norm_post_ffn):
    depth = w_in.shape[0]
    n_p, s_p, _ = x_prompt.shape
    n_s, s_s, _ = x_sample.shape
    n_pages = page_table.shape[1]
    past_len = n_pages * PAGE_SIZE
    n_pool = cache_k.shape[1]

    cache_kt = jnp.transpose(cache_k, (0, 1, 3, 4, 5, 2)).reshape(depth, n_pool, D_MODEL, PAGE_SIZE)
    cache_vr = cache_v.reshape(depth, n_pool, PAGE_SIZE * N_HEADS, V_DIM)
    slopes = jnp.exp2(-8.0 * jnp.arange(1, N_HEADS + 1, dtype=F32) / N_HEADS)
    rslope = jnp.repeat(slopes, 2 * s_s).reshape(N_HEADS * 2 * s_s, 1)
    rqpos = jnp.tile(past_len + jnp.arange(s_s, dtype=jnp.int32), N_HEADS * 2).reshape(N_HEADS * 2 * s_s, 1)

    pool_prev_p = jnp.zeros((n_p, POOL_HALO, D_MODEL), F32)
    ffn_prev_p = jnp.zeros((n_p, CONV_HALO, D_FF), F32)
    pool_prev_s = jnp.pad(state_pool, ((0, 0), (0, 0), (POOL_HALO - POOL_PAD, 0), (0, 0)))
    ffn_prev_s = jnp.pad(state_ffn, ((0, 0), (0, 0), (CONV_HALO - (CONV_W - 1), 0), (0, 0)))

    xp = x_prompt.reshape(n_p * s_p, D_MODEL)
    xs = x_sample.reshape(n_s * s_s, D_MODEL)
    outs = {name: [] for name in ("pp", "fp", "ks", "vs", "ps", "fs")}
    kt_all = v_all = None

    for l in range(depth):
        gn = norm_pre_mix[l].reshape(1, D_MODEL)
        win = w_in[l].astype(BF16)
        wkt = w_in[l, :, 2 * D_MODEL:3 * D_MODEL].T.astype(BF16)
        wpool = w_pool_group[l].astype(BF16)
        pscale = pool_scale[l].reshape(1, D_MODEL)
        lam_params = jnp.stack([lambda_q1[l], lambda_k1[l], lambda_q2[l], lambda_k2[l]])
        sg = subln_gain[l].reshape(1, V_DIM)
        mix_w = (gn, w_gate[l].astype(BF16), b_gate[l].reshape(1, 2 * D_MODEL), w_branch_pool[l].astype(BF16),
                 w_branch_attn[l].astype(BF16), w_out[l].astype(BF16), norm_post_mix[l].reshape(1, D_MODEL))
        ffn_w = (norm_pre_ffn[l].reshape(1, D_MODEL), w_up[l].astype(BF16), conv_w[l], conv_b[l].reshape(1, D_FF),
                 w_down[l].astype(BF16), norm_post_ffn[l].reshape(1, D_MODEL))

        qm, kt_all, ktb, v_all, vb, a, pstate = _in_proj_prompt(
            xp, pool_prev_p, gn, win, wkt, wpool, pscale, kt_all, v_all,
            layer=l, depth=depth, n_seq=n_p, seq_len=s_p)
        o = _prompt_attn(slopes, qm, ktb, vb, lam_params, sg, layer=l, n_seq=n_p, seq_len=s_p)
        x1 = _mix(xp, a, o, *mix_w, tile=ROW_TILE_MIX)
        xp, fstate = _ffn(x1, ffn_prev_p, *ffn_w, n_seq=n_p, seq_len=s_p, G=1, T=ROW_TILE_FFN)
        outs["pp"].append(pstate[:, POOL_HALO - POOL_PAD:, :])
        outs["fp"].append(fstate[:, CONV_HALO - (CONV_W - 1):, :])

        qm, k, v, a, pstate = _in_proj_sample(xs, pool_prev_s[l], gn, win, wpool, pscale,
                                              n_seq=n_s, seq_len=s_s, pos0=past_len)
        qbd = _block_diag_queries(qm, n_s, s_s)
        o = _sample_attn(page_table, qbd, rslope, rqpos, k, v, lam_params, sg, cache_kt, cache_vr,
                         layer=l, n_seq=n_s, L=s_s)
        x1 = _mix(xs, a, o, *mix_w, tile=n_s * s_s)
        xs, fstate = _ffn(x1, ffn_prev_s[l], *ffn_w, n_seq=n_s, seq_len=s_s, G=n_s, T=s_s)
        outs["ks"].append(k.reshape(n_s, s_s, N_HEADS, 2, HEAD_DIM))
        outs["vs"].append(v.reshape(n_s, s_s, N_HEADS, V_DIM))
        outs["ps"].append(pstate[:, POOL_HALO - POOL_PAD:, :])
        outs["fs"].append(fstate[:, CONV_HALO - (CONV_W - 1):, :])

    st = {name: jnp.stack(v) for name, v in outs.items()}
    k_prompt = jnp.transpose(kt_all.reshape(depth, n_p, N_HEADS, 2, HEAD_DIM, s_p), (0, 1, 5, 2, 3, 4))
    v_prompt = v_all.reshape(depth, n_p, s_p, N_HEADS, V_DIM)
    return (xp.reshape(n_p, s_p, D_MODEL), xs.reshape(n_s, s_s, D_MODEL),
            k_prompt, v_prompt, st["pp"], st["fp"], st["ks"], st["vs"], st["ps"], st["fs"])
```

```python
import functools
import math

import jax
import jax.numpy as jnp
from jax import lax
from jax.experimental import pallas as pl
from jax.experimental.pallas import tpu as pltpu

F32 = jnp.float32
BF16 = jnp.bfloat16

D_MODEL = 1024
N_HEADS = 8
HEAD_DIM = 64
V_DIM = 2 * HEAD_DIM
ATTN_SCALE = HEAD_DIM ** -0.5
LOG2E = math.log2(math.e)
POOL_WINDOWS = (2, 4, 8, 16)
POOL_GROUP = D_MODEL // len(POOL_WINDOWS)
POOL_PAD = max(POOL_WINDOWS) - 1
POOL_HALO = 16
D_FF = 2816
CONV_W = 3
CONV_HALO = 8
NORM_EPS = 1e-6
PAGE_SIZE = 128
NEG_BIG = -1e30

VMEM_LIMIT_BYTES = 56 * 1024 * 1024

ROW_TILE_PROJ = 512
ROW_TILE_MIX = 512
ROW_TILE_FFN = 256
ATTN_TILE = 256
PAGES_PER_STEP = 8
PAGES_PER_GROUP = 8


def _lambda_init(layer):
    return 0.8 - 0.6 * math.exp(-0.3 * layer)


def _rms(x, g):
    ms = jnp.mean(x * x, axis=-1, keepdims=True)
    return x * lax.rsqrt(ms + NORM_EPS) * g


def _const_spec(shape):
    nd = len(shape)
    return pl.BlockSpec(shape, lambda *_: (0,) * nd, pipeline_mode=pl.Buffered(1))


def _diff_lambda(lam_ref, layer):
    lp = lam_ref[...]
    a = jnp.sum(lp[0:1, :] * lp[1:2, :], axis=-1, keepdims=True)
    b = jnp.sum(lp[2:3, :] * lp[3:4, :], axis=-1, keepdims=True)
    return jnp.exp(a) - jnp.exp(b) + _lambda_init(layer)


def _store_queries(h, win_ref, qm_ref):
    zq = jnp.dot(h, win_ref[:, D_MODEL:2 * D_MODEL], preferred_element_type=F32) * (ATTN_SCALE * LOG2E)
    lane = lax.broadcasted_iota(jnp.int32, (1, D_MODEL), 1)
    first_map = (lane % V_DIM) < HEAD_DIM
    qm_ref[0] = jnp.where(first_map, zq, 0.0).astype(BF16)
    qm_ref[1] = jnp.where(first_map, 0.0, zq).astype(BF16)


def _pool_branch(t, h, prev_ref, win_ref, wpool_ref, pscale_ref, a_ref, pstate_ref, ext_ref, *, G, T, pos0):
    M = G * T

    @pl.when(t == 0)
    def _():
        ext_ref[:, 0:POOL_HALO, :] = prev_ref[...]

    zu = jnp.dot(h, win_ref[:, 0:D_MODEL], preferred_element_type=F32)
    ext_ref[:, POOL_HALO:POOL_HALO + T, :] = zu.reshape(G, T, D_MODEL)

    pos = pos0 + t * T + lax.broadcasted_iota(jnp.int32, (1, T, 1), 1)
    for g, w in enumerate(POOL_WINDOWS):
        cols = slice(g * POOL_GROUP, (g + 1) * POOL_GROUP)
        u_new = ext_ref[:, POOL_HALO:POOL_HALO + T, cols]
        win = u_new
        for j in range(1, w):
            win = win + ext_ref[:, POOL_HALO - j:POOL_HALO - j + T, cols]
        cnt = jnp.minimum(pos + 1, w).astype(F32)
        d = (win / cnt - u_new).astype(BF16).reshape(M, POOL_GROUP)
        y = jnp.dot(d, wpool_ref[g], preferred_element_type=F32)
        a_ref[:, cols] = (y * pscale_ref[:, cols]).astype(BF16)

    tail = ext_ref[:, T:T + POOL_HALO, :]
    pstate_ref[...] = tail
    ext_ref[:, 0:POOL_HALO, :] = tail


def _in_proj_prompt_kernel(*refs, T, aliased):
    n_in = 7 + (2 if aliased else 0)
    x_ref, prev_ref, gn_ref, win_ref, wkt_ref, wpool_ref, pscale_ref = refs[:7]
    qm_ref, kt_ref, ktb_ref, v_ref, vb_ref, a_ref, pstate_ref, ext_ref = refs[n_in:]
    t = pl.program_id(1)
    h = _rms(x_ref[...], gn_ref[...]).astype(BF16)
    _store_queries(h, win_ref, qm_ref)
    zkt = lax.dot_general(wkt_ref[...], h, (((1,), (1,)), ((), ())), preferred_element_type=F32)
    kt_ref[...] = zkt
    for c in range(T // ATTN_TILE):
        ktb_ref[c] = zkt[:, c * ATTN_TILE:(c + 1) * ATTN_TILE].astype(BF16)
    zv = jnp.dot(h, win_ref[:, 3 * D_MODEL:4 * D_MODEL], preferred_element_type=F32)
    v_ref[...] = zv
    vb_ref[...] = zv.astype(BF16)
    _pool_branch(t, h, prev_ref, win_ref, wpool_ref, pscale_ref, a_ref, pstate_ref, ext_ref, G=1, T=T, pos0=0)


def _in_proj_prompt(x2d, prev, gn, win, wkt, wpool, pscale, kt_all, v_all, *, layer, depth, n_seq, seq_len):
    N = x2d.shape[0]
    T = ROW_TILE_PROJ
    tiles_per_seq = seq_len // T
    blocks_per_tile = T // ATTN_TILE
    aliased = kt_all is not None
    row = lambda b, t: (b * tiles_per_seq + t, 0)
    out_shape = (
        jax.ShapeDtypeStruct((2, N, D_MODEL), BF16),
        jax.ShapeDtypeStruct((depth, n_seq, D_MODEL, seq_len), F32),
        jax.ShapeDtypeStruct((n_seq, seq_len // ATTN_TILE, D_MODEL, ATTN_TILE), BF16),
        jax.ShapeDtypeStruct((depth, N, D_MODEL), F32),
        jax.ShapeDtypeStruct((N, D_MODEL), BF16),
        jax.ShapeDtypeStruct((N, D_MODEL), BF16),
        jax.ShapeDtypeStruct((n_seq, POOL_HALO, D_MODEL), F32),
    )
    in_specs = [
        pl.BlockSpec((T, D_MODEL), row),
        pl.BlockSpec((1, POOL_HALO, D_MODEL), lambda b, t: (b, 0, 0)),
        _const_spec((1, D_MODEL)),
        _const_spec((D_MODEL, 4 * D_MODEL)),
        _const_spec((D_MODEL, D_MODEL)),
        _const_spec((len(POOL_WINDOWS), POOL_GROUP, POOL_GROUP)),
        _const_spec((1, D_MODEL)),
    ]
    args = [x2d, prev, gn, win, wkt, wpool, pscale]
    aliases = {}
    if aliased:
        in_specs += [pl.BlockSpec(memory_space=pl.ANY), pl.BlockSpec(memory_space=pl.ANY)]
        args += [kt_all, v_all]
        aliases = {7: 1, 8: 3}
    return pl.pallas_call(
        functools.partial(_in_proj_prompt_kernel, T=T, aliased=aliased),
        grid=(n_seq, tiles_per_seq),
        in_specs=in_specs,
        out_specs=(
            pl.BlockSpec((2, T, D_MODEL), lambda b, t: (0, b * tiles_per_seq + t, 0)),
            pl.BlockSpec((None, None, D_MODEL, T), lambda b, t: (layer, b, 0, t)),
            pl.BlockSpec((None, blocks_per_tile, D_MODEL, ATTN_TILE), lambda b, t: (b, t, 0, 0)),
            pl.BlockSpec((None, T, D_MODEL), lambda b, t: (layer, b * tiles_per_seq + t, 0)),
            pl.BlockSpec((T, D_MODEL), row),
            pl.BlockSpec((T, D_MODEL), row),
            pl.BlockSpec((1, POOL_HALO, D_MODEL), lambda b, t: (b, 0, 0)),
        ),
        out_shape=out_shape,
        scratch_shapes=[pltpu.VMEM((1, POOL_HALO + T, D_MODEL), F32)],
        input_output_aliases=aliases,
        compiler_params=pltpu.CompilerParams(
            dimension_semantics=("arbitrary", "arbitrary"), vmem_limit_bytes=VMEM_LIMIT_BYTES),
        name="in_proj_prompt",
    )(*args)


def _in_proj_sample_kernel(x_ref, prev_ref, gn_ref, win_ref, wpool_ref, pscale_ref,
                           qm_ref, k_ref, v_ref, a_ref, pstate_ref, ext_ref, *, G, T, pos0):
    h = _rms(x_ref[...], gn_ref[...]).astype(BF16)
    _store_queries(h, win_ref, qm_ref)
    k_ref[...] = jnp.dot(h, win_ref[:, 2 * D_MODEL:3 * D_MODEL], preferred_element_type=F32)
    v_ref[...] = jnp.dot(h, win_ref[:, 3 * D_MODEL:4 * D_MODEL], preferred_element_type=F32)
    _pool_branch(pl.program_id(0), h, prev_ref, win_ref, wpool_ref, pscale_ref, a_ref, pstate_ref, ext_ref,
                 G=G, T=T, pos0=pos0)


def _in_proj_sample(x2d, prev, gn, win, wpool, pscale, *, n_seq, seq_len, pos0):
    N = x2d.shape[0]
    whole2 = lambda i: (0, 0)
    whole3 = lambda i: (0, 0, 0)
    return pl.pallas_call(
        functools.partial(_in_proj_sample_kernel, G=n_seq, T=seq_len, pos0=pos0),
        grid=(1,),
        in_specs=[
            pl.BlockSpec((N, D_MODEL), whole2),
            pl.BlockSpec((n_seq, POOL_HALO, D_MODEL), whole3),
            _const_spec((1, D_MODEL)),
            _const_spec((D_MODEL, 4 * D_MODEL)),
            _const_spec((len(POOL_WINDOWS), POOL_GROUP, POOL_GROUP)),
            _const_spec((1, D_MODEL)),
        ],
        out_specs=(
            pl.BlockSpec((2, N, D_MODEL), whole3),
            pl.BlockSpec((N, D_MODEL), whole2),
            pl.BlockSpec((N, D_MODEL), whole2),
            pl.BlockSpec((N, D_MODEL), whole2),
            pl.BlockSpec((n_seq, POOL_HALO, D_MODEL), whole3),
        ),
        out_shape=(
            jax.ShapeDtypeStruct((2, N, D_MODEL), BF16),
            jax.ShapeDtypeStruct((N, D_MODEL), F32),
            jax.ShapeDtypeStruct((N, D_MODEL), F32),
            jax.ShapeDtypeStruct((N, D_MODEL), BF16),
            jax.ShapeDtypeStruct((n_seq, POOL_HALO, D_MODEL), F32),
        ),
        scratch_shapes=[pltpu.VMEM((n_seq, POOL_HALO + seq_len, D_MODEL), F32)],
        compiler_params=pltpu.CompilerParams(
            dimension_semantics=("arbitrary",), vmem_limit_bytes=VMEM_LIMIT_BYTES),
        name="in_proj_sample",
    )(x2d, prev, gn, win, wpool, pscale)


def _prompt_attn_kernel(slope_ref, qm_ref, kt_ref, vb_ref, lam_ref, sg_ref, o_ref, *scratch, layer, TQ, nq):
    s_ref, macc_ref, mb_ref, lacc_ref, r_ref, acc_ref = (scratch[3 * i:3 * i + 3] for i in range(6))
    hd = pl.program_id(1)
    HALF = TQ // 2
    slope2 = slope_ref[hd] * LOG2E
    key_iota = lax.broadcasted_iota(jnp.int32, (1, TQ), 1)

    def scores(T, slot, j, masked):
        q2 = qm_ref[:, T * TQ:(T + 1) * TQ, :].reshape(2 * TQ, V_DIM)
        s = jnp.dot(q2, kt_ref[j], preferred_element_type=F32)
        s = s + slope2 * ((j - T) * TQ + key_iota).astype(F32)
        if masked:
            qrow = lax.broadcasted_iota(jnp.int32, (2 * TQ, 1), 0) % TQ
            s = jnp.where(key_iota <= qrow, s, NEG_BIG)
        s_ref[slot][j] = s
        macc_ref[slot][...] = jnp.maximum(macc_ref[slot][...], jnp.maximum(s[:, 0:HALF], s[:, HALF:TQ]))

    def probs(slot, j):
        mb = mb_ref[slot][...]
        p_lo = jnp.exp2(s_ref[slot][j, :, 0:HALF] - mb)
        p_hi = jnp.exp2(s_ref[slot][j, :, HALF:TQ] - mb)
        s_ref[slot][j, :, 0:HALF] = p_lo
        s_ref[slot][j, :, HALF:TQ] = p_hi
        lacc_ref[slot][...] += p_lo + p_hi

    def weighted_values(slot, j):
        r0 = r_ref[slot][0:TQ, :]
        r1 = r_ref[slot][TQ:2 * TQ, :]
        a_lo = s_ref[slot][j, 0:TQ, 0:HALF] * r0 - s_ref[slot][j, TQ:2 * TQ, 0:HALF] * r1
        a_hi = s_ref[slot][j, 0:TQ, HALF:TQ] * r0 - s_ref[slot][j, TQ:2 * TQ, HALF:TQ] * r1
        a = jnp.concatenate([a_lo, a_hi], axis=-1).astype(BF16)
        vj = vb_ref[pl.ds(j * TQ if isinstance(j, int) else pl.multiple_of(j * TQ, TQ), TQ), :]
        acc_ref[slot][...] += jnp.dot(a, vj, preferred_element_type=F32)

    def stage(T, do_s, do_p, do_v):
        slot_s, slot_p, slot_v = T % 3, (T + 2) % 3, (T + 1) % 3
        if do_s:
            macc_ref[slot_s][...] = jnp.full(macc_ref[slot_s].shape, NEG_BIG, F32)
        if do_p:
            lacc_ref[slot_p][...] = jnp.zeros(lacc_ref[slot_p].shape, F32)
        if do_v:
            acc_ref[slot_v][...] = jnp.zeros(acc_ref[slot_v].shape, F32)

        def fused(j, c):
            if do_s:
                scores(T, slot_s, j, False)
            if do_p:
                probs(slot_p, j)
            if do_v:
                weighted_values(slot_v, j)
            return c

        if do_v:
            lax.fori_loop(0, T - 1, fused, 0, unroll=True)
        if do_p:
            if do_s:
                scores(T, slot_s, T - 1, False)
            probs(slot_p, T - 1)
        if do_s:
            scores(T, slot_s, T, True)
            m = jnp.max(macc_ref[slot_s][...], axis=-1, keepdims=True)
            mb_ref[slot_s][...] = jnp.broadcast_to(m, mb_ref[slot_s].shape)
        if do_p:
            ones = jnp.ones((HALF, HALF), BF16)
            lsum = jnp.dot(lacc_ref[slot_p][...].astype(BF16), ones, preferred_element_type=F32)
            r_ref[slot_p][0:TQ, :] = 1.0 / lsum[0:TQ]
            r_ref[slot_p][TQ:2 * TQ, :] = _diff_lambda(lam_ref, layer) / lsum[TQ:2 * TQ]
        if do_v:
            o = _rms(acc_ref[slot_v][...], sg_ref[...]) * (1.0 - _lambda_init(layer))
            o_ref[(T - 2) * TQ:(T - 1) * TQ, :] = o.astype(o_ref.dtype)

    for T in range(nq + 2):
        stage(T, T < nq, 1 <= T <= nq, T >= 2)


def _prompt_attn(slopes, qm, ktb, vb, lam_params, subln_gain, *, layer, n_seq, seq_len):
    N = vb.shape[0]
    TQ = ATTN_TILE
    nq = seq_len // TQ
    assert nq >= 2
    return pl.pallas_call(
        functools.partial(_prompt_attn_kernel, layer=layer, TQ=TQ, nq=nq),
        grid=(n_seq, N_HEADS),
        in_specs=[
            pl.BlockSpec(memory_space=pltpu.SMEM),
            pl.BlockSpec((2, seq_len, V_DIM), lambda b, h: (0, b, h)),
            pl.BlockSpec((None, nq, V_DIM, TQ), lambda b, h: (b, 0, h, 0)),
            pl.BlockSpec((seq_len, V_DIM), lambda b, h: (b, h)),
            _const_spec((4, HEAD_DIM)),
            _const_spec((1, V_DIM)),
        ],
        out_specs=pl.BlockSpec((seq_len, V_DIM), lambda b, h: (b, h)),
        out_shape=jax.ShapeDtypeStruct((N, D_MODEL), BF16),
        scratch_shapes=(
            [pltpu.VMEM((nq, 2 * TQ, TQ), F32)] * 3
            + [pltpu.VMEM((2 * TQ, TQ // 2), F32)] * 12
            + [pltpu.VMEM((TQ, V_DIM), F32)] * 3
        ),
        compiler_params=pltpu.CompilerParams(
            dimension_semantics=("arbitrary", "arbitrary"), vmem_limit_bytes=VMEM_LIMIT_BYTES),
        name="prompt_attn",
    )(slopes, qm, ktb, vb, lam_params, subln_gain)


def _sample_attn_kernel(pt_ref, qbd_ref, rslope_ref, rqpos_ref, knew_ref, vnew_ref, lam_ref, sg_ref, *rest,
                        layer, n_pages_step, past_len, L):
    P = n_pages_step
    kpages = rest[:P]
    vpages = rest[P:2 * P]
    o_ref = rest[2 * P]
    m_ref, l_ref, acc_ref, knew_pad_ref, vnew_pad_ref = rest[2 * P + 1:]
    j = pl.program_id(1)
    RH = 2 * L
    qbd = qbd_ref[...]
    rslope2 = rslope_ref[...] * LOG2E
    rqpos = rqpos_ref[...]

    @pl.when(j == 0)
    def _():
        m_ref[...] = jnp.full(m_ref.shape, NEG_BIG, F32)
        l_ref[...] = jnp.zeros(l_ref.shape, F32)
        acc_ref[...] = jnp.zeros(acc_ref.shape, F32)

    def update(state, s, kpos, causal, head_values):
        m_old, l_old, acc_old = state
        dist = (rqpos - kpos).astype(F32)
        s = s - rslope2 * dist
        if causal:
            s = jnp.where(dist >= 0.0, s, NEG_BIG)
        m_new = jnp.maximum(m_old, jnp.max(s, axis=-1, keepdims=True))
        alpha = jnp.exp2(m_old - m_new)
        p = jnp.exp2(s - m_new)
        l_new = alpha * l_old + jnp.sum(p, axis=-1, keepdims=True)
        pb = p.astype(BF16)
        acc_new = []
        for hp in range(N_HEADS // 2):
            rows = slice(2 * hp * RH, (2 * hp + 2) * RH)
            values = jnp.concatenate([head_values(2 * hp), head_values(2 * hp + 1)], axis=1)
            pv = jnp.dot(pb[rows, :], values, preferred_element_type=F32)
            pv = jnp.concatenate([pv[0:RH, 0:V_DIM], pv[RH:2 * RH, V_DIM:2 * V_DIM]], axis=0)
            acc_new.append(alpha[rows, :] * acc_old[rows, :] + pv)
        return m_new, l_new, jnp.concatenate(acc_new, axis=0)

    state = (m_ref[...], l_ref[...], acc_ref[...])
    GP = PAGES_PER_GROUP
    for g in range(P // GP):
        group = range(g * GP, (g + 1) * GP)
        kt = jnp.concatenate([kpages[p_i][...].astype(BF16) for p_i in group], axis=1)
        s = jnp.dot(qbd, kt, preferred_element_type=F32)
        kpos = (j * P + g * GP) * PAGE_SIZE + lax.broadcasted_iota(jnp.int32, (1, GP * PAGE_SIZE), 1)

        def cache_values(hd, group=group):
            rows = [vpages[p_i][pl.ds(hd, PAGE_SIZE, stride=N_HEADS), :] for p_i in group]
            return jnp.concatenate(rows, axis=0).astype(BF16)

        state = update(state, s, kpos, False, cache_values)
    m_ref[...], l_ref[...], acc_ref[...] = state

    @pl.when(j == pl.num_programs(1) - 1)
    def _():
        knew_pad_ref[...] = jnp.zeros(knew_pad_ref.shape, F32)
        vnew_pad_ref[...] = jnp.zeros(vnew_pad_ref.shape, F32)
        knew_pad_ref[0:L, :] = knew_ref[...]
        vnew_pad_ref[0:L, :] = vnew_ref[...]
        s_new = lax.dot_general(qbd, knew_pad_ref[...].astype(BF16), (((1,), (1,)), ((), ())),
                                preferred_element_type=F32)
        kpos_new = past_len + lax.broadcasted_iota(jnp.int32, (1, PAGE_SIZE), 1)
        _, l_fin, acc_fin = update((m_ref[...], l_ref[...], acc_ref[...]), s_new, kpos_new, True,
                                   lambda hd: vnew_pad_ref[:, hd * V_DIM:(hd + 1) * V_DIM].astype(BF16))
        lam = _diff_lambda(lam_ref, layer)
        on = acc_fin / l_fin
        for hd in range(N_HEADS):
            r0 = hd * RH
            o = on[r0:r0 + L, :] - lam * on[r0 + L:r0 + 2 * L, :]
            o = _rms(o, sg_ref[...]) * (1.0 - _lambda_init(layer))
            o_ref[:, hd * V_DIM:(hd + 1) * V_DIM] = o.astype(o_ref.dtype)


def _sample_attn(page_table, qbd, rslope, rqpos, k_new, v_new, lam_params, subln_gain, cache_kt, cache_vr,
                 *, layer, n_seq, L):
    n_pages = page_table.shape[1]
    P = PAGES_PER_STEP
    n_steps = n_pages // P
    R = N_HEADS * 2 * L

    def page_spec(p_i):
        return pl.BlockSpec((None, None, PAGE_SIZE * N_HEADS, V_DIM),
                            lambda b, j, pt: (layer, pt[b, j * P + p_i], 0, 0))

    grid_spec = pltpu.PrefetchScalarGridSpec(
        num_scalar_prefetch=1,
        grid=(n_seq, n_steps),
        in_specs=[
            pl.BlockSpec((None, R, D_MODEL), lambda b, j, pt: (b, 0, 0)),
            pl.BlockSpec((R, 1), lambda b, j, pt: (0, 0)),
            pl.BlockSpec((R, 1), lambda b, j, pt: (0, 0)),
            pl.BlockSpec((L, D_MODEL), lambda b, j, pt: (b, 0)),
            pl.BlockSpec((L, D_MODEL), lambda b, j, pt: (b, 0)),
            pl.BlockSpec((4, HEAD_DIM), lambda b, j, pt: (0, 0)),
            pl.BlockSpec((1, V_DIM), lambda b, j, pt: (0, 0)),
        ] + [page_spec(p_i) for p_i in range(P)] + [page_spec(p_i) for p_i in range(P)],
        out_specs=pl.BlockSpec((L, D_MODEL), lambda b, j, pt: (b, 0)),
        scratch_shapes=[pltpu.VMEM((R, 1), F32), pltpu.VMEM((R, 1), F32), pltpu.VMEM((R, V_DIM), F32),
                        pltpu.VMEM((PAGE_SIZE, D_MODEL), F32), pltpu.VMEM((PAGE_SIZE, D_MODEL), F32)],
    )
    return pl.pallas_call(
        functools.partial(_sample_attn_kernel, layer=layer, n_pages_step=P, past_len=n_pages * PAGE_SIZE, L=L),
        grid_spec=grid_spec,
        out_shape=jax.ShapeDtypeStruct((n_seq * L, D_MODEL), F32),
        compiler_params=pltpu.CompilerParams(
            dimension_semantics=("arbitrary", "arbitrary"), vmem_limit_bytes=VMEM_LIMIT_BYTES),
        name="sample_attn",
    )(page_table, qbd, rslope, rqpos, k_new, v_new, lam_params, subln_gain,
      *([cache_kt] * P), *([cache_vr] * P))


def _mix_kernel(x_ref, a_ref, o_ref, gn_ref, wg_ref, bg_ref, wbp_ref, wba_ref, wout_ref, gpost_ref, y_ref):
    x = x_ref[...]
    h = _rms(x, gn_ref[...]).astype(BF16)
    g = jax.nn.sigmoid(jnp.dot(h, wg_ref[...], preferred_element_type=F32) + bg_ref[...])
    bp = jnp.dot(a_ref[...].astype(BF16), wbp_ref[...], preferred_element_type=F32)
    ba = jnp.dot(o_ref[...].astype(BF16), wba_ref[...], preferred_element_type=F32)
    mix = g[:, 0:D_MODEL] * bp + g[:, D_MODEL:2 * D_MODEL] * ba
    y = jnp.dot(mix.astype(BF16), wout_ref[...], preferred_element_type=F32)
    y_ref[...] = x + _rms(y, gpost_ref[...])


def _mix(x2d, a, o, gn, wg, bg, wbp, wba, wout, gpost, *, tile):
    N = x2d.shape[0]
    row = lambda i: (i, 0)
    return pl.pallas_call(
        _mix_kernel,
        grid=(N // tile,),
        in_specs=[
            pl.BlockSpec((tile, D_MODEL), row),
            pl.BlockSpec((tile, D_MODEL), row),
            pl.BlockSpec((tile, D_MODEL), row),
            _const_spec((1, D_MODEL)),
            _const_spec((D_MODEL, 2 * D_MODEL)),
            _const_spec((1, 2 * D_MODEL)),
            _const_spec((D_MODEL, D_MODEL)),
            _const_spec((D_MODEL, D_MODEL)),
            _const_spec((D_MODEL, D_MODEL)),
            _const_spec((1, D_MODEL)),
        ],
        out_specs=pl.BlockSpec((tile, D_MODEL), row),
        out_shape=jax.ShapeDtypeStruct((N, D_MODEL), F32),
        compiler_params=pltpu.CompilerParams(
            dimension_semantics=("arbitrary",), vmem_limit_bytes=VMEM_LIMIT_BYTES),
        name="mix",
    )(x2d, a, o, gn, wg, bg, wbp, wba, wout, gpost)


def _ffn_kernel(x_ref, prev_ref, gn_ref, wup_ref, cw_ref, cb_ref, wdown_ref, gpost_ref,
                y_ref, fstate_ref, ext_ref, *, G, T):
    t = pl.program_id(1)
    M = G * T
    x = x_ref[...]
    hn = _rms(x, gn_ref[...]).astype(BF16)

    @pl.when(t == 0)
    def _():
        ext_ref[:, 0:CONV_HALO, :] = prev_ref[...]

    gpre = jnp.dot(hn, wup_ref[:, 0:D_FF], preferred_element_type=F32)
    ext_ref[:, CONV_HALO:CONV_HALO + T, :] = gpre.reshape(G, T, D_FF)
    val = jnp.dot(hn, wup_ref[:, D_FF:2 * D_FF], preferred_element_type=F32)

    cw = cw_ref[...]
    c = cb_ref[...].reshape(1, 1, D_FF)
    for j in range(CONV_W):
        off = CONV_HALO - (CONV_W - 1) + j
        c = c + cw[j:j + 1, :].reshape(1, 1, D_FF) * ext_ref[:, off:off + T, :]
    act = jax.nn.gelu(c, approximate=True).reshape(M, D_FF) * val
    f = jnp.dot(act.astype(BF16), wdown_ref[...], preferred_element_type=F32)
    y_ref[...] = x + _rms(f, gpost_ref[...])

    tail = ext_ref[:, T:T + CONV_HALO, :]
    fstate_ref[...] = tail
    ext_ref[:, 0:CONV_HALO, :] = tail


def _ffn(x2d, prev, gn, wup, cw, cb, wdown, gpost, *, n_seq, seq_len, G, T):
    N = x2d.shape[0]
    M = G * T
    tiles_per_seq = seq_len // T
    n_outer = n_seq // G
    row = lambda b, t: (b * tiles_per_seq + t, 0)
    return pl.pallas_call(
        functools.partial(_ffn_kernel, G=G, T=T),
        grid=(n_outer, tiles_per_seq),
        in_specs=[
            pl.BlockSpec((M, D_MODEL), row),
            pl.BlockSpec((G, CONV_HALO, D_FF), lambda b, t: (b, 0, 0)),
            _const_spec((1, D_MODEL)),
            _const_spec((D_MODEL, 2 * D_FF)),
            _const_spec((CONV_W, D_FF)),
            _const_spec((1, D_FF)),
            _const_spec((D_FF, D_MODEL)),
            _const_spec((1, D_MODEL)),
        ],
        out_specs=(
            pl.BlockSpec((M, D_MODEL), row),
            pl.BlockSpec((G, CONV_HALO, D_FF), lambda b, t: (b, 0, 0)),
        ),
        out_shape=(
            jax.ShapeDtypeStruct((N, D_MODEL), F32),
            jax.ShapeDtypeStruct((n_seq, CONV_HALO, D_FF), F32),
        ),
        scratch_shapes=[pltpu.VMEM((G, CONV_HALO + T, D_FF), F32)],
        compiler_params=pltpu.CompilerParams(
            dimension_semantics=("arbitrary", "arbitrary"), vmem_limit_bytes=VMEM_LIMIT_BYTES),
        name="ffn",
    )(x2d, prev, gn, wup, cw, cb, wdown, gpost)


def _block_diag_queries(qm, n_seq, L):
    q = qm.reshape(2, n_seq, L, N_HEADS, V_DIM)
    eye = jnp.eye(N_HEADS, dtype=qm.dtype)
    qbd = jnp.einsum('cbihd,hg->bgcihd', q, eye)
    return qbd.reshape(n_seq, N_HEADS * 2 * L, D_MODEL)


def kernel(x_prompt, x_sample, cache_k, cache_v, state_pool, state_ffn, page_table, norm_pre_mix, w_in,
           w_pool_group, pool_scale, lambda_q1, lambda_k1, lambda_q2, lambda_k2, subln_gain, w_gate, b_gate,
           w_branch_pool, w_branch_attn, w_out, norm_post_mix, norm_pre_ffn, w_up, conv_w, conv_b, w_down,
           norm_post_ffn):
    depth = w_in.shape[0]
    n_p, s_p, _ = x_prompt.shape
    n_s, s_s, _ = x_sample.shape
    n_pages = page_table.shape[1]
    past_len = n_pages * PAGE_SIZE
    n_pool = cache_k.shape[1]

    cache_kt = jnp.transpose(cache_k, (0, 1, 3, 4, 5, 2)).reshape(depth, n_pool, D_MODEL, PAGE_SIZE)
    cache_vr = cache_v.reshape(depth, n_pool, PAGE_SIZE * N_HEADS, V_DIM)
    slopes = jnp.exp2(-8.0 * jnp.arange(1, N_HEADS + 1, dtype=F32) / N_HEADS)
    rslope = jnp.repeat(slopes, 2 * s_s).reshape(N_HEADS * 2 * s_s, 1)
    rqpos = jnp.tile(past_len + jnp.arange(s_s, dtype=jnp.int32), N_HEADS * 2).reshape(N_HEADS * 2 * s_s, 1)

    pool_prev_p = jnp.zeros((n_p, POOL_HALO, D_MODEL), F32)
    ffn_prev_p = jnp.zeros((n_p, CONV_HALO, D_FF), F32)
    pool_prev_s = jnp.pad(state_pool, ((0, 0), (0, 0), (POOL_HALO - POOL_PAD, 0), (0, 0)))
    ffn_prev_s = jnp.pad(state_ffn, ((0, 0), (0, 0), (CONV_HALO - (CONV_W - 1), 0), (0, 0)))

    xp = x_prompt.reshape(n_p * s_p, D_MODEL)
    xs = x_sample.reshape(n_s * s_s, D_MODEL)
    outs = {name: [] for name in ("pp", "fp", "ks", "vs", "ps", "fs")}
    kt_all = v_all = None

    for l in range(depth):
        gn = norm_pre_mix[l].reshape(1, D_MODEL)
        win = w_in[l].astype(BF16)
        wkt = w_in[l, :, 2 * D_MODEL:3 * D_MODEL].T.astype(BF16)
        wpool = w_pool_group[l].astype(BF16)
        pscale = pool_scale[l].reshape(1, D_MODEL)
        lam_params = jnp.stack([lambda_q1[l], lambda_k1[l], lambda_q2[l], lambda_k2[l]])
        sg = subln_gain[l].reshape(1, V_DIM)
        mix_w = (gn, w_gate[l].astype(BF16), b_gate[l].reshape(1, 2 * D_MODEL), w_branch_pool[l].astype(BF16),
                 w_branch_attn[l].astype(BF16), w_out[l].astype(BF16), norm_post_mix[l].reshape(1, D_MODEL))
        ffn_w = (norm_pre_ffn[l].reshape(1, D_MODEL), w_up[l].astype(BF16), conv_w[l], conv_b[l].reshape(1, D_FF),
                 w_down[l].astype(BF16), norm_post_ffn[l].reshape(1, D_MODEL))

        qm, kt_all, ktb, v_all, vb, a, pstate = _in_proj_prompt(
            xp, pool_prev_p, gn, win, wkt, wpool, pscale, kt_all, v_all,
            layer=l, depth=depth, n_seq=n_p, seq_len=s_p)
        o = _prompt_attn(slopes, qm, ktb, vb, lam_params, sg, layer=l, n_seq=n_p, seq_len=s_p)
        x1 = _mix(xp, a, o, *mix_w, tile=ROW_TILE_MIX)
        xp, fstate = _ffn(x1, ffn_prev_p, *ffn_w, n_seq=n_p, seq_len=s_p, G=1, T=ROW_TILE_FFN)
        outs["pp"].append(pstate[:, POOL_HALO - POOL_PAD:, :])
        outs["fp"].append(fstate[:, CONV_HALO - (CONV_W - 1):, :])

        qm, k, v, a, pstate = _in_proj_sample(xs, pool_prev_s[l], gn, win, wpool, pscale,
                                              n_seq=n_s, seq_len=s_s, pos0=past_len)
        qbd = _block_diag_queries(qm, n_s, s_s)
        o = _sample_attn(page_table, qbd, rslope, rqpos, k, v, lam_params, sg, cache_kt, cache_vr,
                         layer=l, n_seq=n_s, L=s_s)
        x1 = _mix(xs, a, o, *mix_w, tile=n_s * s_s)
        xs, fstate = _ffn(x1, ffn_prev_s[l], *ffn_w, n_seq=n_s, seq_len=s_s, G=n_s, T=s_s)
        outs["ks"].append(k.reshape(n_s, s_s, N_HEADS, 2, HEAD_DIM))
        outs["vs"].append(v.reshape(n_s, s_s, N_HEADS, V_DIM))
        outs["ps"].append(pstate[:, POOL_HALO - POOL_PAD:, :])
        outs["fs"].append(fstate[:, CONV_HALO - (CONV_W - 1):, :])

    st = {name: jnp.stack(v) for name, v in outs.items()}
    k_prompt = jnp.transpose(kt_all.reshape(depth, n_p, N_HEADS, 2, HEAD_DIM, s_p), (0, 1, 5, 2, 3, 4))
    v_prompt = v_all.reshape(depth, n_p, s_p, N_HEADS, V_DIM)
    return (xp.reshape(n_p, s_p, D_MODEL), xs.reshape(n_s, s_s, D_MODEL),
            k_prompt, v_prompt, st["pp"], st["fp"], st["ks"], st["vs"], st["ps"], st["fs"])
```
